```python
import jax, jax.numpy as jnp
from jax import lax
import numpy as np

D_MODEL = 4096
BATCH = 2
SEQ = 4096
DEPTH = 2

GRID_W = 64
CTX_LEN = 256
HEAD_DIM = 128
N_HEADS = D_MODEL // 256
N_KV_HEADS = N_HEADS // 4
Q_BLOCK = 128
ROPE_THETA = 10000.0
CONV_W = D_MODEL // 4
CONV_K = 31
FOUR_W = D_MODEL // 4
FOUR_GROUPS = 4
N_BRANCH = 3
N_GROUPS = 4
EXPERTS_PER_GROUP = 4
N_EXPERTS = N_GROUPS * EXPERTS_PER_GROUP
TOP_K = 2
EXPERT_FF = D_MODEL // 4
N_MOD = 6
EPS = 1e-6
MOD_INIT = 0.5

Q_W = N_HEADS * HEAD_DIM
KV_W = N_KV_HEADS * HEAD_DIM
PROJ_SPLITS = (Q_W, Q_W + KV_W, Q_W + 2 * KV_W, Q_W + 2 * KV_W + 2 * CONV_W,
               Q_W + 2 * KV_W + 2 * CONV_W + FOUR_W)
IN_W = PROJ_SPLITS[-1] + N_BRANCH * D_MODEL

kernel_name = "hybrid_gated_conv_fourier_gqa_hmoe_dit"


def rmsnorm(x, g):
    xf = x.astype(jnp.float32)
    y = xf * lax.rsqrt(jnp.mean(xf * xf, axis=-1, keepdims=True) + EPS)
    return (y * g.astype(jnp.float32)).astype(x.dtype)


def layernorm(x, g, b):
    xf = x.astype(jnp.float32)
    mu = jnp.mean(xf, axis=-1, keepdims=True)
    var = jnp.mean(jnp.square(xf - mu), axis=-1, keepdims=True)
    return ((xf - mu) * lax.rsqrt(var + EPS) * g.astype(jnp.float32) + b.astype(jnp.float32)).astype(x.dtype)


def modulate(h, shift, scale):
    return h * (1 + scale) + shift


def ada(cvec, w, b, n):
    m = jax.nn.silu(cvec) @ w[:, : n * D_MODEL] + b[: n * D_MODEL]
    return jnp.split(m, n, axis=-1)


def axial_rope_tables(n_tokens):
    rows = n_tokens // GRID_W
    half = HEAD_DIM // 2
    inv = ROPE_THETA ** (-jnp.arange(0, half, 2, dtype=jnp.float32) / half)
    nf = inv.shape[0]
    row_ang = jnp.broadcast_to(jnp.arange(rows, dtype=jnp.float32)[:, None, None] * inv, (rows, GRID_W, nf))
    col_ang = jnp.broadcast_to(jnp.arange(GRID_W, dtype=jnp.float32)[None, :, None] * inv, (rows, GRID_W, nf))
    ang = jnp.concatenate([row_ang, col_ang], axis=-1).reshape(n_tokens, 2 * nf)
    return jnp.cos(ang), jnp.sin(ang)


def apply_axial_rope(x, cos, sin):
    B, n, H, Dh = x.shape
    xs = x.astype(jnp.float32).reshape(B, n, H, 2, 2, Dh // 4)
    c = cos.reshape(n, 1, 2, Dh // 4)
    s = sin.reshape(n, 1, 2, Dh // 4)
    x1, x2 = xs[..., 0, :], xs[..., 1, :]
    out = jnp.stack([x1 * c - x2 * s, x2 * c + x1 * s], axis=-2)
    return out.reshape(B, n, H, Dh).astype(x.dtype)


def gqa(q, k, v):
    B, nq, H, Dh = q.shape
    qg = q.astype(jnp.float32).reshape(B, nq, N_KV_HEADS, H // N_KV_HEADS, Dh)
    s = jnp.einsum('bqkgd,bskd->bkgqs', qg, k.astype(jnp.float32)) * (Dh ** -0.5)
    p = jax.nn.softmax(s, axis=-1)
    o = jnp.einsum('bkgqs,bskd->bqkgd', p, v.astype(jnp.float32))
    return o.reshape(B, nq, H, Dh).astype(q.dtype)


def blocked_gqa(q, k, v, k_ctx, v_ctx):
    B, n, H, Dh = q.shape
    k_all = jnp.concatenate([k_ctx, k], axis=1).astype(jnp.float32)
    v_all = jnp.concatenate([v_ctx, v], axis=1).astype(jnp.float32)
    qb = q.reshape(B, n // Q_BLOCK, Q_BLOCK, H, Dh).swapaxes(0, 1)
    o = lax.map(lambda qi: gqa(qi, k_all, v_all), qb)
    return o.swapaxes(0, 1).reshape(B, n, H, Dh)


def heads_kv(k, v, k_gain):
    B, n, _ = k.shape
    k = rmsnorm(k.reshape(B, n, N_KV_HEADS, HEAD_DIM), k_gain)
    v = v.reshape(B, n, N_KV_HEADS, HEAD_DIM)
    return k, v


def conformer_conv(a, w_dw, b_dw, ln_g, ln_b):
    u = a[..., :CONV_W] * jax.nn.sigmoid(a[..., CONV_W:])
    y = lax.conv_general_dilated(u, w_dw[:, None, :].astype(u.dtype), window_strides=(1,),
                                 padding=[(CONV_K // 2, CONV_K // 2)],
                                 dimension_numbers=('NWC', 'WIO', 'NWC'),
                                 feature_group_count=CONV_W) + b_dw
    return jax.nn.silu(layernorm(y, ln_g, ln_b))


def fourier_mix(f):
    B, n, _ = f.shape
    fg = f.astype(jnp.float32).reshape(B, n, FOUR_GROUPS, FOUR_W // FOUR_GROUPS)
    y = jnp.fft.fft2(fg, axes=(1, 3), norm='ortho').real
    return y.reshape(B, n, FOUR_W).astype(f.dtype)


def token_mixer(h, w_in, b_gate, q_gain, k_gain, w_dw, b_dw, conv_ln_g, conv_ln_b,
                w_attn_o, w_conv_o, w_four_o, w_out, rope, ctx_kv):
    B, n, _ = h.shape
    q, k, v, a, f, g = jnp.split(h @ w_in, PROJ_SPLITS, axis=-1)
    q = rmsnorm(q.reshape(B, n, N_HEADS, HEAD_DIM), q_gain)
    k, v = heads_kv(k, v, k_gain)
    if rope is None:
        att = gqa(q, k, v)
    else:
        q = apply_axial_rope(q, *rope)
        k = apply_axial_rope(k, *rope)
        att = blocked_gqa(q, k, v, *ctx_kv)
    br_attn = att.reshape(B, n, Q_W) @ w_attn_o
    br_conv = conformer_conv(a, w_dw, b_dw, conv_ln_g, conv_ln_b) @ w_conv_o
    br_four = fourier_mix(f) @ w_four_o
    gates = jax.nn.sigmoid(g + b_gate).reshape(B, n, N_BRANCH, D_MODEL)
    merged = gates[:, :, 0] * br_attn + gates[:, :, 1] * br_conv + gates[:, :, 2] * br_four
    return merged @ w_out, k, v


def hier_moe(h, w_grp, b_grp, w_rt, b_rt, w1, w3, w2):
    shp = h.shape
    x = h.reshape(-1, D_MODEL)
    glog = (x @ w_grp).astype(jnp.float32) + b_grp.astype(jnp.float32)
    gsel = jnp.argmax(glog, axis=-1)
    p_grp = jnp.take_along_axis(jax.nn.softmax(glog, axis=-1), gsel[:, None], axis=-1)
    elog = ((x @ w_rt).astype(jnp.float32) + b_rt.astype(jnp.float32)).reshape(-1, N_GROUPS, EXPERTS_PER_GROUP)
    elog_sel = jnp.take_along_axis(elog, gsel[:, None, None], axis=1)[:, 0]
    top_v, top_i = lax.top_k(elog_sel, TOP_K)
    w_top = jax.nn.softmax(top_v, axis=-1) * p_grp
    eid = gsel[:, None] * EXPERTS_PER_GROUP + top_i
    combine = jnp.sum(jax.nn.one_hot(eid, N_EXPERTS, dtype=jnp.float32) * w_top[..., None], axis=1)
    y = jnp.zeros(x.shape, jnp.float32)
    for e in range(N_EXPERTS):
        he = jax.nn.silu(x @ w1[e]) * (x @ w3[e])
        y = y + combine[:, e:e + 1] * (he @ w2[e]).astype(jnp.float32)
    return y.astype(h.dtype).reshape(shp)


def setup_inputs(seed: int = 0) -> dict:
    key = jax.random.key(seed)
    ks = iter(jax.random.split(key, 32))
    L, D = DEPTH, D_MODEL

    def nrm(shape, scale=1.0):
        return jax.random.normal(next(ks), shape, jnp.float32) * scale

    return {
        "x": nrm((BATCH, SEQ, D)),
        "c": nrm((BATCH, D)),
        "ctx": nrm((BATCH, CTX_LEN, D)),
        "c_ctx": nrm((D,)),
        "w_mod": nrm((L, D, N_MOD * D), MOD_INIT * D ** -0.5),
        "b_mod": nrm((L, N_MOD * D), 0.02),
        "norm1_g": 1.0 + nrm((L, D), 0.02),
        "norm2_g": 1.0 + nrm((L, D), 0.02),
        "w_in": nrm((L, D, IN_W), D ** -0.5),
        "b_gate": nrm((L, N_BRANCH * D), 0.02),
        "q_gain": 1.0 + nrm((L, HEAD_DIM), 0.02),
        "k_gain": 1.0 + nrm((L, HEAD_DIM), 0.02),
        "w_dw": nrm((L, CONV_K, CONV_W), CONV_K ** -0.5),
        "b_dw": nrm((L, CONV_W), 0.02),
        "conv_ln_g": 1.0 + nrm((L, CONV_W), 0.02),
        "conv_ln_b": nrm((L, CONV_W), 0.02),
        "w_attn_o": nrm((L, Q_W, D), Q_W ** -0.5),
        "w_conv_o": nrm((L, CONV_W, D), CONV_W ** -0.5),
        "w_four_o": nrm((L, FOUR_W, D), FOUR_W ** -0.5),
        "w_out": nrm((L, D, D), D ** -0.5),
        "w_grp": nrm((L, D, N_GROUPS), D ** -0.5),
        "b_grp": nrm((L, N_GROUPS), 0.01),
        "w_rt": nrm((L, D, N_EXPERTS), D ** -0.5),
        "b_rt": nrm((L, N_EXPERTS), 0.01),
        "w1": nrm((L, N_EXPERTS, D, EXPERT_FF), D ** -0.5),
        "w3": nrm((L, N_EXPERTS, D, EXPERT_FF), D ** -0.5),
        "w2": nrm((L, N_EXPERTS, EXPERT_FF, D), EXPERT_FF ** -0.5),
        "final_g": 1.0 + nrm((D,), 0.02),
    }


def reference(x, c, ctx, c_ctx, w_mod, b_mod, norm1_g, norm2_g, w_in, b_gate, q_gain, k_gain,
              w_dw, b_dw, conv_ln_g, conv_ln_b, w_attn_o, w_conv_o, w_four_o, w_out,
              w_grp, b_grp, w_rt, b_rt, w1, w3, w2, final_g):
    rope = axial_rope_tables(x.shape[1])
    z, u = x, ctx
    for i in range(DEPTH):
        last = i == DEPTH - 1
        sh1z, sc1z, g1z, sh2z, sc2z, g2z = [t[:, None, :] for t in ada(c, w_mod[i], b_mod[i], N_MOD)]
        mixer_w = (w_in[i], b_gate[i], q_gain[i], k_gain[i], w_dw[i], b_dw[i], conv_ln_g[i], conv_ln_b[i],
                   w_attn_o[i], w_conv_o[i], w_four_o[i], w_out[i])
        moe_w = (w_grp[i], b_grp[i], w_rt[i], b_rt[i], w1[i], w3[i], w2[i])

        if last:
            sh1u, sc1u = ada(c_ctx, w_mod[i], b_mod[i], 2)
            hu = modulate(rmsnorm(u, norm1_g[i]), sh1u, sc1u)
            kv_u = hu @ w_in[i][:, PROJ_SPLITS[0]:PROJ_SPLITS[2]]
            k_u, v_u = heads_kv(kv_u[..., :KV_W], kv_u[..., KV_W:], k_gain[i])
        else:
            sh1u, sc1u, g1u, sh2u, sc2u, g2u = ada(c_ctx, w_mod[i], b_mod[i], N_MOD)
            hu = modulate(rmsnorm(u, norm1_g[i]), sh1u, sc1u)
            out_u, k_u, v_u = token_mixer(hu, *mixer_w, rope=None, ctx_kv=None)

        hz = modulate(rmsnorm(z, norm1_g[i]), sh1z, sc1z)
        out_z, _, _ = token_mixer(hz, *mixer_w, rope=rope, ctx_kv=(k_u, v_u))
        z = z + g1z * out_z
        z = z + g2z * hier_moe(modulate(rmsnorm(z, norm2_g[i]), sh2z, sc2z), *moe_w)

        if not last:
            u = u + g1u * out_u
            u = u + g2u * hier_moe(modulate(rmsnorm(u, norm2_g[i]), sh2u, sc2u), *moe_w)
    return rmsnorm(z, final_g)
```

```python
import functools

import numpy as np
import jax
import jax.numpy as jnp
from jax import lax
from jax.experimental import pallas as pl
from jax.experimental.pallas import tpu as pltpu

F32 = jnp.float32
BF16 = jnp.bfloat16

EPS = 1e-6
GRID_W = 64
ROPE_THETA = 10000.0
TOP_K = 2
FOUR_GROUPS = 4
N_MOD = 6
CONV_HALO = 16
ROUTER_LANES = 128

V7X_VMEM_LIMIT_BYTES = 56 * 1024 * 1024


def _pcall(body, *, grid, in_specs, out_specs, out_shape, scratch_shapes=(), name=None,
           num_scalar_prefetch=0):
    params = pltpu.CompilerParams(dimension_semantics=("arbitrary",) * len(grid),
                                  vmem_limit_bytes=V7X_VMEM_LIMIT_BYTES)
    if num_scalar_prefetch:
        grid_spec = pltpu.PrefetchScalarGridSpec(
            num_scalar_prefetch=num_scalar_prefetch, grid=grid, in_specs=in_specs,
            out_specs=out_specs, scratch_shapes=scratch_shapes)
        return pl.pallas_call(body, grid_spec=grid_spec, out_shape=out_shape,
                              compiler_params=params, name=name)
    return pl.pallas_call(body, grid=grid, in_specs=in_specs, out_specs=out_specs,
                          out_shape=out_shape, scratch_shapes=scratch_shapes,
                          compiler_params=params, name=name)


def _pick(n, *cands):
    for c in cands:
        if n % c == 0:
            return c
    return n


def _ada(cs_t, w, b, n_vec):
    D, N = w.shape
    tk = _pick(D, 512, 256, 128)
    tn = _pick(N, 2048, 1024, 512, 256, 128)

    def body(c_ref, w_ref, b_ref, o_ref):
        @pl.when(pl.program_id(1) == 0)
        def _():
            o_ref[...] = jnp.broadcast_to(b_ref[...], o_ref.shape)

        c = c_ref[...]
        s = c * jax.nn.sigmoid(c)
        wv = w_ref[...]
        rows = [jnp.sum(s[:, r:r + 1] * wv, axis=0, keepdims=True) for r in range(n_vec)]
        rows.append(jnp.zeros((8 - n_vec, tn), F32))
        o_ref[...] += jnp.concatenate(rows, axis=0)

    return _pcall(
        body, grid=(N // tn, D // tk),
        in_specs=[pl.BlockSpec((tk, 8), lambda j, k: (k, 0)),
                  pl.BlockSpec((tk, tn), lambda j, k: (k, j)),
                  pl.BlockSpec((1, tn), lambda j, k: (0, j))],
        out_specs=pl.BlockSpec((8, tn), lambda j, k: (0, j)),
        out_shape=jax.ShapeDtypeStruct((8, N), F32), name="ada")(cs_t, w, b)


def _seg_of_row(row0, seq, n_batch):
    return jnp.minimum(row0 // seq, n_batch)


def _mod_spec(idx, tr, width, seq, n_batch):
    return pl.BlockSpec((None, 1, width),
                        lambda i: (_seg_of_row(i * tr, seq, n_batch) * N_MOD + idx, 0, 0))


def _norm_mod(x, g, mods, sh_idx, sc_idx, *, m, seq, n_batch):
    D = x.shape[1]
    tr = _pick(m, 256, 128)

    def body(x_ref, g_ref, sh_ref, sc_ref, o_ref):
        xf = x_ref[...]
        y = xf * lax.rsqrt(jnp.mean(xf * xf, axis=-1, keepdims=True) + EPS) * g_ref[...]
        o_ref[...] = (y * (1.0 + sc_ref[...]) + sh_ref[...]).astype(o_ref.dtype)

    return _pcall(
        body, grid=(m // tr,),
        in_specs=[pl.BlockSpec((tr, D), lambda i: (i, 0)),
                  pl.BlockSpec((1, D), lambda i: (0, 0)),
                  _mod_spec(sh_idx, tr, D, seq, n_batch),
                  _mod_spec(sc_idx, tr, D, seq, n_batch)],
        out_specs=pl.BlockSpec((tr, D), lambda i: (i, 0)),
        out_shape=jax.ShapeDtypeStruct((m, D), BF16), name="norm_mod")(x, g, mods, mods)


def _norm_mod_router(x, g, mods, sh_idx, sc_idx, w_router, b_router, *, m, seq, n_batch):
    D = x.shape[1]
    tr = _pick(m, 256, 128)

    def body(x_ref, g_ref, sh_ref, sc_ref, wr_ref, br_ref, o_ref, l_ref):
        xf = x_ref[...]
        y = xf * lax.rsqrt(jnp.mean(xf * xf, axis=-1, keepdims=True) + EPS) * g_ref[...]
        h = y * (1.0 + sc_ref[...]) + sh_ref[...]
        o_ref[...] = h
        l_ref[...] = jnp.dot(h, wr_ref[...], precision=lax.Precision.HIGHEST,
                             preferred_element_type=F32) + br_ref[...]

    return _pcall(
        body, grid=(m // tr,),
        in_specs=[pl.BlockSpec((tr, D), lambda i: (i, 0)),
                  pl.BlockSpec((1, D), lambda i: (0, 0)),
                  _mod_spec(sh_idx, tr, D, seq, n_batch),
                  _mod_spec(sc_idx, tr, D, seq, n_batch),
                  pl.BlockSpec((D, ROUTER_LANES), lambda i: (0, 0)),
                  pl.BlockSpec((1, ROUTER_LANES), lambda i: (0, 0))],
        out_specs=[pl.BlockSpec((tr, D), lambda i: (i, 0)),
                   pl.BlockSpec((tr, ROUTER_LANES), lambda i: (i, 0))],
        out_shape=[jax.ShapeDtypeStruct((m, D), F32),
                   jax.ShapeDtypeStruct((m, ROUTER_LANES), F32)],
        name="norm_mod_router")(x, g, mods, mods, w_router, b_router)


def _final_norm(x, g, *, m):
    D = x.shape[1]
    tr = _pick(m, 256, 128)

    def body(x_ref, g_ref, o_ref):
        xf = x_ref[...]
        o_ref[...] = xf * lax.rsqrt(jnp.mean(xf * xf, axis=-1, keepdims=True) + EPS) * g_ref[...]

    return _pcall(
        body, grid=(m // tr,),
        in_specs=[pl.BlockSpec((tr, D), lambda i: (i, 0)),
                  pl.BlockSpec((1, D), lambda i: (0, 0))],
        out_specs=pl.BlockSpec((tr, D), lambda i: (i, 0)),
        out_shape=jax.ShapeDtypeStruct((m, D), F32), name="final_norm")(x, g)


def _matmul(xs, ws, extras, epilogue, *, m, n, tm, tn, out_dtype, name, seq=1, n_batch=0):
    in_specs, args = [], []
    for x in xs:
        in_specs.append(pl.BlockSpec((tm, x.shape[1]), lambda j, i: (i, 0)))
        args.append(x)
    for w, _, off in ws:
        in_specs.append(pl.BlockSpec((w.shape[0], tn), lambda j, i, off=off: (0, j + off)))
        args.append(w)
    for ex in extras:
        kind, arr = ex[0], ex[1]
        if kind == "col":
            in_specs.append(pl.BlockSpec((1, tn), lambda j, i, off=ex[2]: (0, j + off)))
        elif kind == "tile":
            in_specs.append(pl.BlockSpec((tm, tn), lambda j, i: (i, j)))
        elif kind == "rows":
            in_specs.append(pl.BlockSpec((tm, arr.shape[1]), lambda j, i: (i, 0)))
        elif kind == "full":
            in_specs.append(pl.BlockSpec(arr.shape, lambda j, i: (0, 0)))
        elif kind == "mod":
            in_specs.append(pl.BlockSpec(
                (None, 1, tn),
                lambda j, i, idx=ex[2]: (_seg_of_row(i * tm, seq, n_batch) * N_MOD + idx, 0, j)))
        args.append(arr)
    nx, nw = len(xs), len(ws)

    def body(*refs):
        x_refs, w_refs = refs[:nx], refs[nx:nx + nw]
        e_refs, o_ref = refs[nx + nw:-1], refs[-1]
        xv = [r[...] for r in x_refs]
        dots = [jnp.dot(xv[xi], w_refs[p][...], preferred_element_type=F32)
                for p, (_, xi, _) in enumerate(ws)]
        o_ref[...] = epilogue(dots, *[r[...] for r in e_refs]).astype(o_ref.dtype)

    return _pcall(body, grid=(n // tn, m // tm), in_specs=in_specs,
                  out_specs=pl.BlockSpec((tm, tn), lambda j, i: (i, j)),
                  out_shape=jax.ShapeDtypeStruct((m, n), out_dtype), name=name)(*args)


def _head_norm_rope_epilogue(head_dim, scale):
    quarter = head_dim // 4

    def epilogue(dots, gain, cos, sin_signed):
        d = dots[0]
        lane = lax.broadcasted_iota(jnp.int32, (d.shape[0], head_dim), 1)
        first_half = (lane % (2 * quarter)) < quarter
        outs = []
        for h in range(d.shape[1] // head_dim):
            xh = d[:, h * head_dim:(h + 1) * head_dim]
            y = xh * lax.rsqrt(jnp.mean(xh * xh, axis=-1, keepdims=True) + EPS) * gain
            partner = jnp.where(first_half, pltpu.roll(y, head_dim - quarter, 1),
                                pltpu.roll(y, quarter, 1))
            outs.append((y * cos + partner * sin_signed) * scale)
        return jnp.concatenate(outs, axis=-1)

    return epilogue


def _softmax_pv(q, keys, values):
    dn = (((1,), (1,)), ((), ()))
    s = [lax.dot_general(q, k, dn, preferred_element_type=F32) for k in keys]
    m = s[0].max(axis=-1, keepdims=True)
    for si in s[1:]:
        m = jnp.maximum(m, si.max(axis=-1, keepdims=True))
    p = [jnp.exp(si - m) for si in s]
    l = p[0].sum(axis=-1, keepdims=True)
    for pi in p[1:]:
        l = l + pi.sum(axis=-1, keepdims=True)
    o = jnp.dot(p[0].astype(BF16), values[0], preferred_element_type=F32)
    for pi, v in zip(p[1:], values[1:]):
        o = o + jnp.dot(pi.astype(BF16), v, preferred_element_type=F32)
    return o / l


def _attn_latent(q, k, v, *, n_batch, seq, ctx_len, n_kv, grp, hd):
    tq = _pick(seq, 256, 128)
    qt = seq // tq
    ctx_blk0 = n_batch * seq // ctx_len

    def body(q_ref, kl_ref, vl_ref, kc_ref, vc_ref, o_ref):
        keys, values = (kc_ref[...], kl_ref[...]), (vc_ref[...], vl_ref[...])
        for g in range(grp):
            o = _softmax_pv(q_ref[:, g * hd:(g + 1) * hd], keys, values)
            o_ref[:, g * hd:(g + 1) * hd] = o.astype(o_ref.dtype)

    lat_spec = pl.BlockSpec((seq, hd), lambda b, h, i: (b, h))
    ctx_spec = pl.BlockSpec((ctx_len, hd), lambda b, h, i: (ctx_blk0 + b, h))
    return _pcall(
        body, grid=(n_batch, n_kv, qt),
        in_specs=[pl.BlockSpec((tq, grp * hd), lambda b, h, i: (b * qt + i, h)),
                  lat_spec, lat_spec, ctx_spec, ctx_spec],
        out_specs=pl.BlockSpec((tq, grp * hd), lambda b, h, i: (b * qt + i, h)),
        out_shape=jax.ShapeDtypeStruct((n_batch * seq, n_kv * grp * hd), BF16),
        name="attn_latent")(q, k, v, k, v)


def _attn_context(q, k, v, *, n_batch, seq, ctx_len, n_kv, grp, hd):
    ctx_blk0 = n_batch * seq // ctx_len

    def body(q_ref, kc_ref, vc_ref, o_ref):
        keys, values = (kc_ref[...],), (vc_ref[...],)
        for g in range(grp):
            o = _softmax_pv(q_ref[:, g * hd:(g + 1) * hd], keys, values)
            o_ref[:, g * hd:(g + 1) * hd] = o.astype(o_ref.dtype)

    ctx_spec = pl.BlockSpec((ctx_len, hd), lambda b, h: (ctx_blk0 + b, h))
    return _pcall(
        body, grid=(n_batch, n_kv),
        in_specs=[pl.BlockSpec((ctx_len, grp * hd), lambda b, h: (ctx_blk0 + b, h)),
                  ctx_spec, ctx_spec],
        out_specs=pl.BlockSpec((ctx_len, grp * hd), lambda b, h: (b, h)),
        out_shape=jax.ShapeDtypeStruct((n_batch * ctx_len, n_kv * grp * hd), BF16),
        name="attn_context")(q, k, v)


def _conv_ln_silu(u, w_dw, b_dw, ln_g, ln_b, *, m, seq, ctx_len, n_lat_rows):
    cw = u.shape[1]
    taps = w_dw.shape[0]
    half = taps // 2
    ts = ctx_len
    halo_per_tile = ts // CONV_HALO
    n_halo_blocks = m // CONV_HALO
    lat_tiles = n_lat_rows // ts
    tiles_per_seq = seq // ts
    rc = 16

    def body(u_ref, prev_ref, next_ref, w_ref, b_ref, g_ref, beta_ref, o_ref, win_ref):
        t = pl.program_id(0)
        in_lat = t < lat_tiles
        pos = t % tiles_per_seq
        is_first = jnp.logical_or(jnp.logical_not(in_lat), pos == 0)
        is_last = jnp.logical_or(jnp.logical_not(in_lat), pos == tiles_per_seq - 1)
        win_ref[0:CONV_HALO, :] = jnp.where(is_first, 0.0, prev_ref[...])
        win_ref[CONV_HALO:CONV_HALO + ts, :] = u_ref[...]
        win_ref[CONV_HALO + ts:, :] = jnp.where(is_last, 0.0, next_ref[...])
        wv = w_ref[...]
        for c0 in range(0, ts, rc):
            acc = jnp.zeros((rc, cw), F32)
            for kk in range(taps):
                r = c0 + CONV_HALO - half + kk
                acc = acc + win_ref[r:r + rc, :] * wv[kk:kk + 1, :]
            y = acc + b_ref[...]
            mu = jnp.mean(y, axis=-1, keepdims=True)
            yc = y - mu
            var = jnp.mean(yc * yc, axis=-1, keepdims=True)
            z = yc * lax.rsqrt(var + EPS) * g_ref[...] + beta_ref[...]
            o_ref[c0:c0 + rc, :] = (z * jax.nn.sigmoid(z)).astype(o_ref.dtype)

    vec = pl.BlockSpec((1, cw), lambda t: (0, 0))
    return _pcall(
        body, grid=(m // ts,),
        in_specs=[pl.BlockSpec((ts, cw), lambda t: (t, 0)),
                  pl.BlockSpec((CONV_HALO, cw),
                               lambda t: (jnp.maximum(t * halo_per_tile - 1, 0), 0)),
                  pl.BlockSpec((CONV_HALO, cw),
                               lambda t: (jnp.minimum((t + 1) * halo_per_tile, n_halo_blocks - 1), 0)),
                  pl.BlockSpec((taps, cw), lambda t: (0, 0)), vec, vec, vec],
        out_specs=pl.BlockSpec((ts, cw), lambda t: (t, 0)),
        out_shape=jax.ShapeDtypeStruct((m, cw), BF16),
        scratch_shapes=[pltpu.VMEM((ts + 2 * CONV_HALO, cw), F32)],
        name="conv_ln_silu")(u, u, u, w_dw, b_dw, ln_g, ln_b)


def _dft_tables(n):
    j = np.arange(n, dtype=np.int64)[:, None]
    if n % 64 or n <= 64:
        ang = 2.0 * np.pi * ((j * j.T) % n) / n
        return jnp.asarray(np.cos(ang), BF16), jnp.asarray(np.sin(ang), BF16)
    lo = 64
    hi = n // lo
    a_hi = 2.0 * np.pi * ((j * lo * np.arange(hi, dtype=np.int64)[None, :]) % n) / n
    a_lo = 2.0 * np.pi * ((j * np.arange(lo, dtype=np.int64)[None, :]) % n) / n
    ch, sh = jnp.asarray(np.cos(a_hi), F32)[:, :, None], jnp.asarray(np.sin(a_hi), F32)[:, :, None]
    cl, sl = jnp.asarray(np.cos(a_lo), F32)[:, None, :], jnp.asarray(np.sin(a_lo), F32)[:, None, :]
    c = (ch * cl - sh * sl).reshape(n, n).astype(BF16)
    s = (sh * cl + ch * sl).reshape(n, n).astype(BF16)
    return c, s


def _channel_dft_matrix(width, groups):
    gw = width // groups
    k = np.arange(gw, dtype=np.int64)
    ang = 2.0 * np.pi * ((k[:, None] * k[None, :]) % gw) / gw
    out = np.zeros((width, 2 * width), np.float32)
    for g in range(groups):
        sl = slice(g * gw, (g + 1) * gw)
        out[sl, sl] = np.cos(ang)
        out[sl, width + g * gw: width + (g + 1) * gw] = -np.sin(ang)
    return jnp.asarray(out, BF16)


def _position_dft(xcs, cn, sn, *, n, row_blk0, n_batch, width, scale):
    tm = _pick(n, 512, 256, 128)
    tn = _pick(width, 512, 256, 128)
    mt = n // tm
    im_off = width // tn

    def body(c_ref, s_ref, re_ref, im_ref, o_ref):
        acc = jnp.dot(c_ref[...], re_ref[...], preferred_element_type=F32)
        acc = acc + jnp.dot(s_ref[...], im_ref[...], preferred_element_type=F32)
        o_ref[...] = (acc * scale).astype(o_ref.dtype)

    mat_spec = pl.BlockSpec((tm, n), lambda b, j, i: (i, 0))
    return _pcall(
        body, grid=(n_batch, width // tn, mt),
        in_specs=[mat_spec, mat_spec,
                  pl.BlockSpec((n, tn), lambda b, j, i: (row_blk0 + b, j)),
                  pl.BlockSpec((n, tn), lambda b, j, i: (row_blk0 + b, j + im_off))],
        out_specs=pl.BlockSpec((tm, tn), lambda b, j, i: (b * mt + i, j)),
        out_shape=jax.ShapeDtypeStruct((n_batch * n, width), BF16),
        name="position_dft")(cn, sn, xcs, xcs)


def _route(logits, n_groups, per_group):
    glog = logits[:, :n_groups]
    gsel = jnp.argmax(glog, axis=-1)
    p_grp = jnp.take_along_axis(jax.nn.softmax(glog, axis=-1), gsel[:, None], axis=-1)
    elog = logits[:, n_groups:n_groups + n_groups * per_group].reshape(-1, n_groups, per_group)
    elog_sel = jnp.take_along_axis(elog, gsel[:, None, None], axis=1)[:, 0]
    top_v, top_i = lax.top_k(elog_sel, TOP_K)
    w_top = jax.nn.softmax(top_v, axis=-1) * p_grp
    eid = gsel[:, None] * per_group + top_i
    return eid.astype(jnp.int32), w_top


def _dispatch_plan(eid, w_top, n_experts, tm):
    m = eid.shape[0]
    n_slots = m * TOP_K
    n_tiles = n_slots // tm + n_experts
    flat_e = eid.reshape(-1)
    order = jnp.argsort(flat_e, stable=True).astype(jnp.int32)
    sorted_e = flat_e[order]
    counts = jnp.sum(jax.nn.one_hot(flat_e, n_experts, dtype=jnp.int32), axis=0)
    padded = ((counts + tm - 1) // tm) * tm
    pad_end = jnp.cumsum(padded)
    pad_start = pad_end - padded
    start = jnp.cumsum(counts) - counts
    ppos = pad_start[sorted_e] + jnp.arange(n_slots, dtype=jnp.int32) - start[sorted_e]
    row_tok = jnp.zeros((n_tiles * tm,), jnp.int32).at[ppos].set(order // TOP_K)
    row_w = jnp.zeros((n_tiles * tm,), F32).at[ppos].set(w_top.reshape(-1)[order])
    slot_pos = jnp.zeros((n_slots,), jnp.int32).at[order].set(ppos)
    tile_row0 = jnp.arange(n_tiles, dtype=jnp.int32) * tm
    tile_e = jnp.minimum(jnp.searchsorted(pad_end, tile_row0, side="right"),
                         n_experts - 1).astype(jnp.int32)
    tile_valid = (tile_row0 < pad_end[-1]).astype(jnp.int32)
    return row_tok, row_w[:, None], slot_pos, tile_e, tile_valid, n_tiles


def _experts(xm, w1, w3, w2, row_tok, row_w, tile_e, tile_valid, *, n_tiles, tm):
    D = xm.shape[1]
    ff = w1.shape[2]
    fh = ff // 2

    def half_of(i, h):
        return jnp.where(i % 2 == 0, h, 1 - h)

    def body(te_ref, tv_ref, tok_ref, x_hbm, w1_ref, w3_ref, w2_ref, rw_ref, o_ref,
             xg_ref, xb_ref, sem):
        i, h = pl.program_id(0), pl.program_id(1)

        def row_copy(tile, r):
            tok = tok_ref[tile * tm + r]
            return pltpu.make_async_copy(x_hbm.at[pl.ds(tok, 1), :], xg_ref.at[pl.ds(r, 1), :],
                                         sem.at[0])

        def start_gather(tile):
            def issue(r, carry):
                row_copy(tile, r).start()
                return carry
            lax.fori_loop(0, tm, issue, 0)

        def wait_gather(tile):
            def drain(r, carry):
                row_copy(tile, r).wait()
                return carry
            lax.fori_loop(0, tm, drain, 0)

        valid = tv_ref[i] == 1

        @pl.when(jnp.logical_and(i == 0, h == 0))
        def _():
            start_gather(0)

        @pl.when(jnp.logical_and(h == 0, valid))
        def _():
            wait_gather(i)
            xb_ref[...] = xg_ref[...].astype(BF16)

        @pl.when(jnp.logical_and(h == 0, i + 1 < n_tiles))
        def _():
            @pl.when(tv_ref[jnp.minimum(i + 1, n_tiles - 1)] == 1)
            def _():
                start_gather(i + 1)

        @pl.when(valid)
        def _():
            x = xb_ref[...]
            a = jnp.dot(x, w1_ref[...], preferred_element_type=F32)
            b = jnp.dot(x, w3_ref[...], preferred_element_type=F32)
            hid = (a * jax.nn.sigmoid(a) * b).astype(BF16)
            y = jnp.dot(hid, w2_ref[...], preferred_element_type=F32)

            @pl.when(h == 0)
            def _():
                o_ref[...] = y

            @pl.when(h == 1)
            def _():
                o_ref[...] = (o_ref[...] + y) * rw_ref[...]

        @pl.when(jnp.logical_and(jnp.logical_not(valid), h == 0))
        def _():
            o_ref[...] = jnp.zeros(o_ref.shape, o_ref.dtype)

    return _pcall(
        body, grid=(n_tiles, 2), num_scalar_prefetch=3,
        in_specs=[pl.BlockSpec(memory_space=pl.ANY),
                  pl.BlockSpec((None, D, fh), lambda i, h, te, tv, tok: (te[i], 0, half_of(i, h))),
                  pl.BlockSpec((None, D, fh), lambda i, h, te, tv, tok: (te[i], 0, half_of(i, h))),
                  pl.BlockSpec((None, fh, D), lambda i, h, te, tv, tok: (te[i], half_of(i, h), 0)),
                  pl.BlockSpec((tm, 1), lambda i, h, te, tv, tok: (i, 0))],
        out_specs=pl.BlockSpec((tm, D), lambda i, h, te, tv, tok: (i, 0)),
        out_shape=jax.ShapeDtypeStruct((n_tiles * tm, D), F32),
        scratch_shapes=[pltpu.VMEM((tm, D), F32), pltpu.VMEM((tm, D), BF16),
                        pltpu.SemaphoreType.DMA((1,))],
        name="experts")(tile_e, tile_valid, row_tok, xm, w1, w3, w2, row_w)


def _combine(y_sorted, slot_pos, z, mods, gate_idx, *, m, seq, n_batch):
    D = z.shape[1]
    tc = _pick(m, 256, 128)
    n_steps = m // tc

    def body(pos_ref, y_hbm, z_ref, g_ref, o_ref, buf_ref, sem):
        i = pl.program_id(0)

        def row_copy(step, slot, r, k):
            src = pos_ref[(step * tc + r) * TOP_K + k]
            return pltpu.make_async_copy(y_hbm.at[pl.ds(src, 1), :],
                                         buf_ref.at[slot, k, pl.ds(r, 1), :], sem.at[slot])

        def start_gather(step, slot):
            def issue(r, carry):
                for k in range(TOP_K):
                    row_copy(step, slot, r, k).start()
                return carry
            lax.fori_loop(0, tc, issue, 0)

        def wait_gather(step, slot):
            def drain(r, carry):
                for k in range(TOP_K):
                    row_copy(step, slot, r, k).wait()
                return carry
            lax.fori_loop(0, tc, drain, 0)

        slot = i % 2

        @pl.when(i == 0)
        def _():
            start_gather(0, 0)

        @pl.when(i + 1 < n_steps)
        def _():
            start_gather(i + 1, 1 - slot)

        wait_gather(i, slot)
        y = buf_ref[slot, 0]
        for k in range(1, TOP_K):
            y = y + buf_ref[slot, k]
        o_ref[...] = z_ref[...] + g_ref[...] * y

    return _pcall(
        body, grid=(n_steps,), num_scalar_prefetch=1,
        in_specs=[pl.BlockSpec(memory_space=pl.ANY),
                  pl.BlockSpec((tc, D), lambda i, pos: (i, 0)),
                  pl.BlockSpec((None, 1, D),
                               lambda i, pos: (_seg_of_row(i * tc, seq, n_batch) * N_MOD + gate_idx, 0, 0))],
        out_specs=pl.BlockSpec((tc, D), lambda i, pos: (i, 0)),
        out_shape=jax.ShapeDtypeStruct((m, D), F32),
        scratch_shapes=[pltpu.VMEM((2, TOP_K, tc, D), F32), pltpu.SemaphoreType.DMA((2,))],
        name="moe_combine")(slot_pos, y_sorted, z, mods)


def _rope_tables(seq, head_dim, n_batch, n_ctx_rows):
    half = head_dim // 2
    inv = ROPE_THETA ** (-jnp.arange(0, half, 2, dtype=F32) / half)
    t = jnp.arange(seq)
    row_ang = (t // GRID_W).astype(F32)[:, None] * inv
    col_ang = (t % GRID_W).astype(F32)[:, None] * inv
    cr, sr, cc, sc = jnp.cos(row_ang), jnp.sin(row_ang), jnp.cos(col_ang), jnp.sin(col_ang)
    cos = jnp.concatenate([cr, cr, cc, cc], axis=-1)
    sin = jnp.concatenate([-sr, sr, -sc, sc], axis=-1)
    cos = jnp.concatenate([jnp.tile(cos, (n_batch, 1)), jnp.ones((n_ctx_rows, head_dim), F32)])
    sin = jnp.concatenate([jnp.tile(sin, (n_batch, 1)), jnp.zeros((n_ctx_rows, head_dim), F32)])
    return cos, sin


def kernel(x, c, ctx, c_ctx, w_mod, b_mod, norm1_g, norm2_g, w_in, b_gate, q_gain, k_gain,
           w_dw, b_dw, conv_ln_g, conv_ln_b, w_attn_o, w_conv_o, w_four_o, w_out,
           w_grp, b_grp, w_rt, b_rt, w1, w3, w2, final_g):
    n_batch, seq, D = x.shape
    ctx_len = ctx.shape[1]
    depth = w_mod.shape[0]
    hd = q_gain.shape[-1]
    q_w = w_attn_o.shape[1]
    conv_w = w_conv_o.shape[1]
    four_w = w_four_o.shape[1]
    kv_w = (w_in.shape[2] - q_w - 2 * conv_w - four_w - 3 * D) // 2
    n_kv = kv_w // hd
    grp = q_w // kv_w
    n_groups = w_grp.shape[2]
    n_experts = w_rt.shape[2]
    per_group = n_experts // n_groups
    o_k, o_v, o_a, o_f, o_g = q_w, q_w + kv_w, q_w + 2 * kv_w, q_w + 2 * kv_w + 2 * conv_w, \
        q_w + 2 * kv_w + 2 * conv_w + four_w

    m_lat = n_batch * seq
    m_ctx = n_batch * ctx_len
    m_all = m_lat + m_ctx
    tm = _pick(ctx_len * n_batch, 512, 256, 128)
    assert seq % tm == 0 and m_ctx % tm == 0 and seq % ctx_len == 0 and seq % GRID_W == 0
    n_vec = n_batch + 1
    assert n_vec <= 8

    z = jnp.concatenate([x.reshape(m_lat, D), ctx.reshape(m_ctx, D)], axis=0)
    cs_t = jnp.zeros((D, 8), F32).at[:, :n_batch].set(c.T).at[:, n_batch].set(c_ctx)
    cos_t, sin_t = _rope_tables(seq, hd, n_batch, m_ctx)
    cn_lat, sn_lat = _dft_tables(seq)
    cn_ctx, sn_ctx = _dft_tables(ctx_len)
    chan_dft = _channel_dft_matrix(four_w, FOUR_GROUPS)
    gw = four_w // FOUR_GROUPS
    seg = dict(seq=seq, n_batch=n_batch)
    plain = lambda dots: dots[0]

    for l in range(depth):
        last = l == depth - 1
        m = m_lat if last else m_all
        wl = w_in[l]
        mods = _ada(cs_t, w_mod[l], b_mod[l][None, :], n_vec)[:n_vec]
        mods = mods.reshape(n_vec * N_MOD, 1, D)

        h = _norm_mod(z, norm1_g[l][None, :], mods, 0, 1, m=m_all, **seg)

        tn_h = _pick(kv_w, 512, 256, 128)
        q = _matmul([h], [(wl[:, :o_k].astype(BF16), 0, 0)],
                    [("full", q_gain[l][None, :]), ("rows", cos_t), ("rows", sin_t)],
                    _head_norm_rope_epilogue(hd, hd ** -0.5),
                    m=m, n=q_w, tm=tm, tn=tn_h, out_dtype=BF16, name="q_proj")
        k = _matmul([h], [(wl[:, o_k:o_v].astype(BF16), 0, 0)],
                    [("full", k_gain[l][None, :]), ("rows", cos_t), ("rows", sin_t)],
                    _head_norm_rope_epilogue(hd, 1.0),
                    m=m_all, n=kv_w, tm=tm, tn=tn_h, out_dtype=BF16, name="k_proj")
        v = _matmul([h], [(wl[:, o_v:o_a].astype(BF16), 0, 0)], [], plain,
                    m=m_all, n=kv_w, tm=tm, tn=tn_h, out_dtype=BF16, name="v_proj")
        geo = dict(n_batch=n_batch, seq=seq, ctx_len=ctx_len, n_kv=n_kv, grp=grp, hd=hd)
        att = _attn_latent(q, k, v, **geo)
        if not last:
            att = jnp.concatenate([att, _attn_context(q, k, v, **geo)], axis=0)

        tn_c = _pick(conv_w, 512, 256, 128)
        w_glu = wl[:, o_a:o_f].astype(BF16)
        u = _matmul([h], [(w_glu, 0, 0), (w_glu, 0, conv_w // tn_c)], [],
                    lambda dots: dots[0] * jax.nn.sigmoid(dots[1]),
                    m=m, n=conv_w, tm=tm, tn=tn_c, out_dtype=F32, name="glu_proj")
        conv = _conv_ln_silu(u, w_dw[l], b_dw[l][None, :], conv_ln_g[l][None, :],
                             conv_ln_b[l][None, :], m=m, seq=seq, ctx_len=ctx_len,
                             n_lat_rows=m_lat)

        tn_f = _pick(four_w, 512, 256, 128)
        f = _matmul([h], [(wl[:, o_f:o_g].astype(BF16), 0, 0)], [], plain,
                    m=m, n=four_w, tm=tm, tn=tn_f, out_dtype=BF16, name="four_proj")
        xcs = _matmul([f], [(chan_dft, 0, 0)], [], plain,
                      m=m, n=2 * four_w, tm=tm, tn=tn_f, out_dtype=BF16, name="channel_dft")
        four = _position_dft(xcs, cn_lat, sn_lat, n=seq, row_blk0=0, n_batch=n_batch,
                             width=four_w, scale=float((seq * gw) ** -0.5))
        if not last:
            four_ctx = _position_dft(xcs, cn_ctx, sn_ctx, n=ctx_len, row_blk0=m_lat // ctx_len,
                                     n_batch=n_batch, width=four_w,
                                     scale=float((ctx_len * gw) ** -0.5))
            four = jnp.concatenate([four, four_ctx], axis=0)

        tn_m = _pick(D, 256, 128)
        w_gate = wl[:, o_g:].astype(BF16)
        d_t = D // tn_m

        def merge(dots, b0, b1, b2):
            return (jax.nn.sigmoid(dots[0] + b0) * dots[1] + jax.nn.sigmoid(dots[2] + b1) * dots[3]
                    + jax.nn.sigmoid(dots[4] + b2) * dots[5])

        bg = b_gate[l][None, :]
        merged = _matmul(
            [h, att, conv, four],
            [(w_gate, 0, 0), (w_attn_o[l].astype(BF16), 1, 0),
             (w_gate, 0, d_t), (w_conv_o[l].astype(BF16), 2, 0),
             (w_gate, 0, 2 * d_t), (w_four_o[l].astype(BF16), 3, 0)],
            [("col", bg, 0), ("col", bg, d_t), ("col", bg, 2 * d_t)], merge,
            m=m, n=D, tm=tm, tn=tn_m, out_dtype=BF16, name="gated_merge")

        tn_o = _pick(D, 1024, 512, 256, 128)
        z1 = _matmul([merged], [(w_out[l].astype(BF16), 0, 0)],
                     [("tile", z), ("mod", mods, 2)],
                     lambda dots, zt, g: zt + g * dots[0],
                     m=m, n=D, tm=tm, tn=tn_o, out_dtype=F32, name="out_proj", **seg)

        w_router = jnp.zeros((D, ROUTER_LANES), F32)
        w_router = w_router.at[:, :n_groups].set(w_grp[l]).at[:, n_groups:n_groups + n_experts].set(w_rt[l])
        b_router = jnp.zeros((1, ROUTER_LANES), F32)
        b_router = b_router.at[0, :n_groups].set(b_grp[l]).at[0, n_groups:n_groups + n_experts].set(b_rt[l])
        xm, logits = _norm_mod_router(z1, norm2_g[l][None, :], mods, 3, 4, w_router, b_router,
                                      m=m, **seg)
        eid, w_top = _route(logits, n_groups, per_group)
        tm_e = _pick(m * TOP_K, 256, 128)
        row_tok, row_w, slot_pos, tile_e, tile_valid, n_tiles = _dispatch_plan(
            eid, w_top, n_experts, tm_e)
        y_sorted = _experts(xm, w1[l].astype(BF16), w3[l].astype(BF16), w2[l].astype(BF16),
                            row_tok, row_w, tile_e, tile_valid, n_tiles=n_tiles, tm=tm_e)
        z = _combine(y_sorted, slot_pos, z1, mods, 5, m=m, **seg)

    out = _final_norm(z, final_g[None, :], m=m_lat)
    return out.reshape(n_batch, seq, D)
```

```python
import math

import numpy as np
import jax
import jax.numpy as jnp
from jax import lax
from jax.experimental import pallas as pl
from jax.experimental.pallas import tpu as pltpu

F32 = jnp.float32
BF16 = jnp.bfloat16
U32 = jnp.uint32

EPS = 1e-6
GRID_W = 64
ROPE_THETA = 10000.0
TOP_K = 2
FOUR_GROUPS = 4
N_MOD = 6
CONV_HALO = 16
ROUTER_LANES = 128
FF_SPLITS = 4

V7X_VMEM_LIMIT_BYTES = 56 * 1024 * 1024


def _pcall(body, *, grid, in_specs, out_specs, out_shape, scratch_shapes=(), name=None,
           num_scalar_prefetch=0, input_output_aliases=None):
    params = pltpu.CompilerParams(dimension_semantics=("arbitrary",) * len(grid),
                                  vmem_limit_bytes=V7X_VMEM_LIMIT_BYTES)
    aliases = input_output_aliases or {}
    if num_scalar_prefetch:
        grid_spec = pltpu.PrefetchScalarGridSpec(
            num_scalar_prefetch=num_scalar_prefetch, grid=grid, in_specs=in_specs,
            out_specs=out_specs, scratch_shapes=scratch_shapes)
        return pl.pallas_call(body, grid_spec=grid_spec, out_shape=out_shape,
                              compiler_params=params, name=name, input_output_aliases=aliases)
    return pl.pallas_call(body, grid=grid, in_specs=in_specs, out_specs=out_specs,
                          out_shape=out_shape, scratch_shapes=scratch_shapes,
                          compiler_params=params, name=name, input_output_aliases=aliases)


def _pick(n, *cands):
    for c in cands:
        if n % c == 0:
            return c
    return n


def _pack_halves(x):
    half = x.shape[1] // 2
    lo = lax.bitcast_convert_type(x[:, :half].astype(BF16).astype(F32), U32)
    hi = lax.bitcast_convert_type(x[:, half:].astype(BF16).astype(F32), U32)
    return hi | (lo >> 16)


def _unpack_halves(p):
    lo = lax.bitcast_convert_type(p << 16, F32)
    hi = lax.bitcast_convert_type(p & jnp.uint32(0xFFFF0000), F32)
    return lo, hi


def _ada(cs_t, w, b, layer, n_vec):
    _, D, N = w.shape
    tk = _pick(D, 512, 256, 128)
    tn = _pick(N, 2048, 1024, 512, 256, 128)

    def body(c_ref, w_ref, b_ref, o_ref):
        @pl.when(pl.program_id(1) == 0)
        def _():
            o_ref[...] = jnp.broadcast_to(b_ref[...], o_ref.shape)

        c = c_ref[...]
        s = c * jax.nn.sigmoid(c)
        wv = w_ref[...]
        rows = [jnp.sum(s[:, r:r + 1] * wv, axis=0, keepdims=True) for r in range(n_vec)]
        rows.append(jnp.zeros((8 - n_vec, tn), F32))
        o_ref[...] += jnp.concatenate(rows, axis=0)

    return _pcall(
        body, grid=(N // tn, D // tk),
        in_specs=[pl.BlockSpec((tk, 8), lambda j, k: (k, 0)),
                  pl.BlockSpec((None, tk, tn), lambda j, k: (layer, k, j)),
                  pl.BlockSpec((None, 1, tn), lambda j, k: (layer, 0, j))],
        out_specs=pl.BlockSpec((8, tn), lambda j, k: (0, j)),
        out_shape=jax.ShapeDtypeStruct((8, N), F32), name="ada")(cs_t, w, b)


def _seg_of_row(row0, seq, n_batch):
    return jnp.minimum(row0 // seq, n_batch)


def _mod_spec(idx, tr, width, seq, n_batch):
    return pl.BlockSpec((None, 1, width),
                        lambda i, *_: (_seg_of_row(i * tr, seq, n_batch) * N_MOD + idx, 0, 0))


def _norm_mod(x, g, mods, sh_idx, sc_idx, *, m, seq, n_batch):
    D = x.shape[1]
    tr = _pick(m, 256, 128)

    def body(x_ref, g_ref, sh_ref, sc_ref, o_ref):
        xf = x_ref[...]
        y = xf * lax.rsqrt(jnp.mean(xf * xf, axis=-1, keepdims=True) + EPS) * g_ref[...]
        o_ref[...] = (y * (1.0 + sc_ref[...]) + sh_ref[...]).astype(o_ref.dtype)

    return _pcall(
        body, grid=(m // tr,),
        in_specs=[pl.BlockSpec((tr, D), lambda i: (i, 0)),
                  pl.BlockSpec((1, D), lambda i: (0, 0)),
                  _mod_spec(sh_idx, tr, D, seq, n_batch),
                  _mod_spec(sc_idx, tr, D, seq, n_batch)],
        out_specs=pl.BlockSpec((tr, D), lambda i: (i, 0)),
        out_shape=jax.ShapeDtypeStruct((m, D), BF16), name="norm_mod")(x, g, mods, mods)


def _norm_mod_router(x, g, mods, sh_idx, sc_idx, w_router, b_router, *, m, seq, n_batch):
    D = x.shape[1]
    tr = _pick(m, 256, 128)

    def body(x_ref, g_ref, sh_ref, sc_ref, wr_ref, br_ref, o_ref, l_ref):
        xf = x_ref[...]
        y = xf * lax.rsqrt(jnp.mean(xf * xf, axis=-1, keepdims=True) + EPS) * g_ref[...]
        h = y * (1.0 + sc_ref[...]) + sh_ref[...]
        o_ref[...] = _pack_halves(h)
        l_ref[...] = jnp.dot(h, wr_ref[...], precision=lax.Precision.HIGHEST,
                             preferred_element_type=F32) + br_ref[...]

    return _pcall(
        body, grid=(m // tr,),
        in_specs=[pl.BlockSpec((tr, D), lambda i: (i, 0)),
                  pl.BlockSpec((1, D), lambda i: (0, 0)),
                  _mod_spec(sh_idx, tr, D, seq, n_batch),
                  _mod_spec(sc_idx, tr, D, seq, n_batch),
                  pl.BlockSpec((D, ROUTER_LANES), lambda i: (0, 0)),
                  pl.BlockSpec((1, ROUTER_LANES), lambda i: (0, 0))],
        out_specs=[pl.BlockSpec((tr, D // 2), lambda i: (i, 0)),
                   pl.BlockSpec((tr, ROUTER_LANES), lambda i: (i, 0))],
        out_shape=[jax.ShapeDtypeStruct((m, D // 2), U32),
                   jax.ShapeDtypeStruct((m, ROUTER_LANES), F32)],
        name="norm_mod_router")(x, g, mods, mods, w_router, b_router)


def _final_norm(x, g, *, m):
    D = x.shape[1]
    tr = _pick(m, 256, 128)

    def body(x_ref, g_ref, o_ref):
        xf = x_ref[...]
        o_ref[...] = xf * lax.rsqrt(jnp.mean(xf * xf, axis=-1, keepdims=True) + EPS) * g_ref[...]

    return _pcall(
        body, grid=(m // tr,),
        in_specs=[pl.BlockSpec((tr, D), lambda i: (i, 0)),
                  pl.BlockSpec((1, D), lambda i: (0, 0))],
        out_specs=pl.BlockSpec((tr, D), lambda i: (i, 0)),
        out_shape=jax.ShapeDtypeStruct((m, D), F32), name="final_norm")(x, g)


def _matmul(xs, ws, extras, epilogue, *, m, n, tm, tn, out_dtype, name, seq=1, n_batch=0):
    in_specs, args = [], []
    for x in xs:
        in_specs.append(pl.BlockSpec((tm, x.shape[1]), lambda j, i: (i, 0)))
        args.append(x)
    for w, layer, _, off in ws:
        in_specs.append(pl.BlockSpec((None, w.shape[1], tn),
                                     lambda j, i, layer=layer, off=off: (layer, 0, j + off)))
        args.append(w)
    for ex in extras:
        kind, arr = ex[0], ex[1]
        if kind == "col":
            in_specs.append(pl.BlockSpec((1, tn), lambda j, i, off=ex[2]: (0, j + off)))
        elif kind == "tile":
            in_specs.append(pl.BlockSpec((tm, tn), lambda j, i: (i, j)))
        elif kind == "rows":
            in_specs.append(pl.BlockSpec((tm, arr.shape[1]), lambda j, i: (i, 0)))
        elif kind == "full":
            in_specs.append(pl.BlockSpec(arr.shape, lambda j, i: (0, 0)))
        elif kind == "mod":
            in_specs.append(pl.BlockSpec(
                (None, 1, tn),
                lambda j, i, idx=ex[2]: (_seg_of_row(i * tm, seq, n_batch) * N_MOD + idx, 0, j)))
        args.append(arr)
    nx, nw = len(xs), len(ws)

    def body(*refs):
        x_refs, w_refs = refs[:nx], refs[nx:nx + nw]
        e_refs, o_ref = refs[nx + nw:-1], refs[-1]
        xv = [r[...] for r in x_refs]
        dots = [jnp.dot(xv[xi], w_refs[p][...], preferred_element_type=F32)
                for p, (_, _, xi, _) in enumerate(ws)]
        o_ref[...] = epilogue(dots, *[r[...] for r in e_refs]).astype(o_ref.dtype)

    return _pcall(body, grid=(n // tn, m // tm), in_specs=in_specs,
                  out_specs=pl.BlockSpec((tm, tn), lambda j, i: (i, j)),
                  out_shape=jax.ShapeDtypeStruct((m, n), out_dtype), name=name)(*args)


def _head_norm_rope_epilogue(head_dim, scale):
    quarter = head_dim // 4

    def epilogue(dots, gain, cos, sin_signed):
        d = dots[0]
        lane = lax.broadcasted_iota(jnp.int32, (d.shape[0], head_dim), 1)
        first_half = (lane % (2 * quarter)) < quarter
        outs = []
        for h in range(d.shape[1] // head_dim):
            xh = d[:, h * head_dim:(h + 1) * head_dim]
            y = xh * lax.rsqrt(jnp.mean(xh * xh, axis=-1, keepdims=True) + EPS) * gain
            partner = jnp.where(first_half, pltpu.roll(y, head_dim - quarter, 1),
                                pltpu.roll(y, quarter, 1))
            outs.append((y * cos + partner * sin_signed) * scale)
        return jnp.concatenate(outs, axis=-1)

    return epilogue


def _softmax_pv(q, keys, values):
    dn = (((1,), (1,)), ((), ()))
    s = [lax.dot_general(q, k, dn, preferred_element_type=F32) for k in keys]
    m = s[0].max(axis=-1, keepdims=True)
    for si in s[1:]:
        m = jnp.maximum(m, si.max(axis=-1, keepdims=True))
    p = [jnp.exp2(si - m) for si in s]
    l = p[0].sum(axis=-1, keepdims=True)
    for pi in p[1:]:
        l = l + pi.sum(axis=-1, keepdims=True)
    o = jnp.dot(p[0].astype(BF16), values[0], preferred_element_type=F32)
    for pi, v in zip(p[1:], values[1:]):
        o = o + jnp.dot(pi.astype(BF16), v, preferred_element_type=F32)
    return o / l


def _attn_latent(q, k, v, *, n_batch, seq, ctx_len, n_kv, grp, hd):
    tq = _pick(seq, 256, 128)
    qt = seq // tq
    ctx_blk0 = n_batch * seq // ctx_len

    def body(q_ref, kl_ref, vl_ref, kc_ref, vc_ref, o_ref):
        keys, values = (kc_ref[...], kl_ref[...]), (vc_ref[...], vl_ref[...])
        for g in range(grp):
            o = _softmax_pv(q_ref[:, g * hd:(g + 1) * hd], keys, values)
            o_ref[:, g * hd:(g + 1) * hd] = o.astype(o_ref.dtype)

    lat_spec = pl.BlockSpec((seq, hd), lambda b, h, i: (b, h))
    ctx_spec = pl.BlockSpec((ctx_len, hd), lambda b, h, i: (ctx_blk0 + b, h))
    return _pcall(
        body, grid=(n_batch, n_kv, qt),
        in_specs=[pl.BlockSpec((tq, grp * hd), lambda b, h, i: (b * qt + i, h)),
                  lat_spec, lat_spec, ctx_spec, ctx_spec],
        out_specs=pl.BlockSpec((tq, grp * hd), lambda b, h, i: (b * qt + i, h)),
        out_shape=jax.ShapeDtypeStruct((n_batch * seq, n_kv * grp * hd), BF16),
        name="attn_latent")(q, k, v, k, v)


def _attn_context(q, k, v, *, n_batch, seq, ctx_len, n_kv, grp, hd):
    ctx_blk0 = n_batch * seq // ctx_len

    def body(q_ref, kc_ref, vc_ref, o_ref):
        keys, values = (kc_ref[...],), (vc_ref[...],)
        for g in range(grp):
            o = _softmax_pv(q_ref[:, g * hd:(g + 1) * hd], keys, values)
            o_ref[:, g * hd:(g + 1) * hd] = o.astype(o_ref.dtype)

    ctx_spec = pl.BlockSpec((ctx_len, hd), lambda b, h: (ctx_blk0 + b, h))
    return _pcall(
        body, grid=(n_batch, n_kv),
        in_specs=[pl.BlockSpec((ctx_len, grp * hd), lambda b, h: (ctx_blk0 + b, h)),
                  ctx_spec, ctx_spec],
        out_specs=pl.BlockSpec((ctx_len, grp * hd), lambda b, h: (b, h)),
        out_shape=jax.ShapeDtypeStruct((n_batch * ctx_len, n_kv * grp * hd), BF16),
        name="attn_context")(q, k, v)


def _conv_ln_silu(u, w_dw, b_dw, ln_g, ln_b, *, m, seq, ctx_len, n_lat_rows):
    cw = u.shape[1]
    taps = w_dw.shape[0]
    half = taps // 2
    ts = ctx_len
    halo_per_tile = ts // CONV_HALO
    n_halo_blocks = m // CONV_HALO
    lat_tiles = n_lat_rows // ts
    tiles_per_seq = seq // ts
    rc = 16

    def body(u_ref, prev_ref, next_ref, w_ref, b_ref, g_ref, beta_ref, o_ref, win_ref):
        t = pl.program_id(0)
        in_lat = t < lat_tiles
        pos = t % tiles_per_seq
        is_first = jnp.logical_or(jnp.logical_not(in_lat), pos == 0)
        is_last = jnp.logical_or(jnp.logical_not(in_lat), pos == tiles_per_seq - 1)
        win_ref[0:CONV_HALO, :] = jnp.where(is_first, 0.0, prev_ref[...])
        win_ref[CONV_HALO:CONV_HALO + ts, :] = u_ref[...]
        win_ref[CONV_HALO + ts:, :] = jnp.where(is_last, 0.0, next_ref[...])
        wv = w_ref[...]
        for c0 in range(0, ts, rc):
            acc = jnp.zeros((rc, cw), F32)
            for kk in range(taps):
                r = c0 + CONV_HALO - half + kk
                acc = acc + win_ref[r:r + rc, :] * wv[kk:kk + 1, :]
            y = acc + b_ref[...]
            mu = jnp.mean(y, axis=-1, keepdims=True)
            yc = y - mu
            var = jnp.mean(yc * yc, axis=-1, keepdims=True)
            z = yc * lax.rsqrt(var + EPS) * g_ref[...] + beta_ref[...]
            o_ref[c0:c0 + rc, :] = (z * jax.nn.sigmoid(z)).astype(o_ref.dtype)

    vec = pl.BlockSpec((1, cw), lambda t: (0, 0))
    return _pcall(
        body, grid=(m // ts,),
        in_specs=[pl.BlockSpec((ts, cw), lambda t: (t, 0)),
                  pl.BlockSpec((CONV_HALO, cw),
                               lambda t: (jnp.maximum(t * halo_per_tile - 1, 0), 0)),
                  pl.BlockSpec((CONV_HALO, cw),
                               lambda t: (jnp.minimum((t + 1) * halo_per_tile, n_halo_blocks - 1), 0)),
                  pl.BlockSpec((taps, cw), lambda t: (0, 0)), vec, vec, vec],
        out_specs=pl.BlockSpec((ts, cw), lambda t: (t, 0)),
        out_shape=jax.ShapeDtypeStruct((m, cw), BF16),
        scratch_shapes=[pltpu.VMEM((ts + 2 * CONV_HALO, cw), F32)],
        name="conv_ln_silu")(u, u, u, w_dw, b_dw, ln_g, ln_b)


def _dft_tables(n):
    j = np.arange(n, dtype=np.int64)[:, None]
    if n % 64 or n <= 64:
        ang = 2.0 * np.pi * ((j * j.T) % n) / n
        return jnp.asarray(np.cos(ang), BF16), jnp.asarray(np.sin(ang), BF16)
    lo = 64
    hi = n // lo
    a_hi = 2.0 * np.pi * ((j * lo * np.arange(hi, dtype=np.int64)[None, :]) % n) / n
    a_lo = 2.0 * np.pi * ((j * np.arange(lo, dtype=np.int64)[None, :]) % n) / n
    ch, sh = jnp.asarray(np.cos(a_hi), F32)[:, :, None], jnp.asarray(np.sin(a_hi), F32)[:, :, None]
    cl, sl = jnp.asarray(np.cos(a_lo), F32)[:, None, :], jnp.asarray(np.sin(a_lo), F32)[:, None, :]
    c = (ch * cl - sh * sl).reshape(n, n).astype(BF16)
    s = (sh * cl + ch * sl).reshape(n, n).astype(BF16)
    return c, s


def _channel_dft_matrix(width, groups):
    gw = width // groups
    k = np.arange(gw, dtype=np.int64)
    ang = 2.0 * np.pi * ((k[:, None] * k[None, :]) % gw) / gw
    out = np.zeros((width, 2 * width), np.float32)
    for g in range(groups):
        sl = slice(g * gw, (g + 1) * gw)
        out[sl, sl] = np.cos(ang)
        out[sl, width + g * gw: width + (g + 1) * gw] = -np.sin(ang)
    return jnp.asarray(out, BF16)[None]


def _position_dft(xcs, cn, sn, *, n, row_blk0, n_batch, width, scale):
    tm = _pick(n, 512, 256, 128)
    tn = _pick(width, 512, 256, 128)
    mt = n // tm
    im_off = width // tn

    def body(c_ref, s_ref, re_ref, im_ref, o_ref):
        acc = jnp.dot(c_ref[...], re_ref[...], preferred_element_type=F32)
        acc = acc + jnp.dot(s_ref[...], im_ref[...], preferred_element_type=F32)
        o_ref[...] = (acc * scale).astype(o_ref.dtype)

    mat_spec = pl.BlockSpec((tm, n), lambda b, j, i: (i, 0))
    return _pcall(
        body, grid=(n_batch, width // tn, mt),
        in_specs=[mat_spec, mat_spec,
                  pl.BlockSpec((n, tn), lambda b, j, i: (row_blk0 + b, j)),
                  pl.BlockSpec((n, tn), lambda b, j, i: (row_blk0 + b, j + im_off))],
        out_specs=pl.BlockSpec((tm, tn), lambda b, j, i: (b * mt + i, j)),
        out_shape=jax.ShapeDtypeStruct((n_batch * n, width), BF16),
        name="position_dft")(cn, sn, xcs, xcs)


def _route(logits, n_groups, per_group):
    glog = logits[:, :n_groups]
    g_hot = jax.nn.one_hot(jnp.argmax(glog, axis=-1), n_groups, dtype=F32)
    p_grp = jnp.sum(jax.nn.softmax(glog, axis=-1) * g_hot, axis=-1, keepdims=True)
    elog = logits[:, n_groups:n_groups + n_groups * per_group].reshape(-1, n_groups, per_group)
    elog_sel = jnp.sum(elog * g_hot[:, :, None], axis=1)
    i0 = jnp.argmax(elog_sel, axis=-1)
    hot0 = jax.nn.one_hot(i0, per_group, dtype=jnp.bool_)
    rest = jnp.where(hot0, -jnp.inf, elog_sel)
    i1 = jnp.argmax(rest, axis=-1)
    top_v = jnp.stack([jnp.max(elog_sel, axis=-1), jnp.max(rest, axis=-1)], axis=-1)
    w_top = jax.nn.softmax(top_v, axis=-1) * p_grp
    gsel = jnp.argmax(glog, axis=-1)
    eid = gsel[:, None] * per_group + jnp.stack([i0, i1], axis=-1)
    return eid.astype(jnp.int32), w_top


def _dispatch_plan(eid, n_experts, tm):
    n_slots = eid.shape[0] * TOP_K
    n_tiles = n_slots // tm + n_experts
    hot = (eid.reshape(-1)[:, None] == jnp.arange(n_experts, dtype=jnp.int32)[None, :])
    hot = hot.astype(jnp.int32)
    running = jnp.cumsum(hot, axis=0)
    counts = running[-1]
    padded = ((counts + tm - 1) // tm) * tm
    pad_end = jnp.cumsum(padded)
    pad_start = pad_end - padded
    slot_pos = jnp.sum(hot * (running - 1 + pad_start[None, :]), axis=1).astype(jnp.int32)
    tile_row0 = jnp.arange(n_tiles, dtype=jnp.int32) * tm
    tile_e = jnp.sum((tile_row0[:, None] >= pad_end[None, :]).astype(jnp.int32), axis=1)
    tile_e = jnp.minimum(tile_e, n_experts - 1).astype(jnp.int32)
    tile_valid = (tile_row0 < pad_end[-1]).astype(jnp.int32)
    return slot_pos, tile_e, tile_valid, n_tiles


def _dispatch(xp, slot_pos, *, n_rows_out):
    m, dh = xp.shape
    tr = _pick(m, 256, 128)

    def body(pos_ref, x_ref, init_hbm, o_hbm, sem):
        del init_hbm
        i = pl.program_id(0)

        def row_copy(r, k):
            dst = pos_ref[(i * tr + r) * TOP_K + k]
            return pltpu.make_async_copy(x_ref.at[pl.ds(r, 1), :], o_hbm.at[pl.ds(dst, 1), :],
                                         sem.at[0])

        def issue(r, carry):
            for k in range(TOP_K):
                row_copy(r, k).start()
            return carry

        def drain(r, carry):
            for k in range(TOP_K):
                row_copy(r, k).wait()
            return carry

        lax.fori_loop(0, tr, issue, 0)
        lax.fori_loop(0, tr, drain, 0)

    return _pcall(
        body, grid=(m // tr,), num_scalar_prefetch=1,
        in_specs=[pl.BlockSpec((tr, dh), lambda i, pos: (i, 0)),
                  pl.BlockSpec(memory_space=pl.ANY)],
        out_specs=pl.BlockSpec(memory_space=pl.ANY),
        out_shape=jax.ShapeDtypeStruct((n_rows_out, dh), U32),
        scratch_shapes=[pltpu.SemaphoreType.DMA((1,))],
        input_output_aliases={2: 0},
        name="moe_dispatch")(slot_pos, xp, jnp.zeros((n_rows_out, dh), U32))


def _experts(xs, w1, w3, w2, layer, tile_e, tile_valid, *, n_tiles, tm):
    dh = xs.shape[1]
    D = 2 * dh
    fq = w1.shape[3] // FF_SPLITS

    def blk(i, q):
        return jnp.where(i % 2 == 0, q, FF_SPLITS - 1 - q)

    def body(te_ref, tv_ref, x_ref, w1_ref, w3_ref, w2_ref, o_ref, xlo_ref, xhi_ref, acc_ref):
        i, q = pl.program_id(0), pl.program_id(1)
        valid = tv_ref[i] == 1

        @pl.when(jnp.logical_and(valid, q == 0))
        def _():
            lo, hi = _unpack_halves(x_ref[...])
            xlo_ref[...] = lo.astype(BF16)
            xhi_ref[...] = hi.astype(BF16)

        @pl.when(valid)
        def _():
            xlo, xhi = xlo_ref[...], xhi_ref[...]
            a = (jnp.dot(xlo, w1_ref[:dh, :], preferred_element_type=F32)
                 + jnp.dot(xhi, w1_ref[dh:, :], preferred_element_type=F32))
            b = (jnp.dot(xlo, w3_ref[:dh, :], preferred_element_type=F32)
                 + jnp.dot(xhi, w3_ref[dh:, :], preferred_element_type=F32))
            hid = (a * jax.nn.sigmoid(a) * b).astype(BF16)
            y = jnp.dot(hid, w2_ref[...], preferred_element_type=F32)

            @pl.when(q == 0)
            def _():
                acc_ref[...] = y

            @pl.when(q > 0)
            def _():
                acc_ref[...] += y

        @pl.when(q == FF_SPLITS - 1)
        def _():
            @pl.when(valid)
            def _():
                o_ref[...] = _pack_halves(acc_ref[...])

            @pl.when(jnp.logical_not(valid))
            def _():
                o_ref[...] = jnp.zeros(o_ref.shape, o_ref.dtype)

    return _pcall(
        body, grid=(n_tiles, FF_SPLITS), num_scalar_prefetch=2,
        in_specs=[pl.BlockSpec((tm, dh), lambda i, q, te, tv: (i, 0)),
                  pl.BlockSpec((None, None, D, fq), lambda i, q, te, tv: (layer, te[i], 0, blk(i, q))),
                  pl.BlockSpec((None, None, D, fq), lambda i, q, te, tv: (layer, te[i], 0, blk(i, q))),
                  pl.BlockSpec((None, None, fq, D), lambda i, q, te, tv: (layer, te[i], blk(i, q), 0))],
        out_specs=pl.BlockSpec((tm, dh), lambda i, q, te, tv: (i, 0)),
        out_shape=jax.ShapeDtypeStruct((n_tiles * tm, dh), U32),
        scratch_shapes=[pltpu.VMEM((tm, dh), BF16), pltpu.VMEM((tm, dh), BF16),
                        pltpu.VMEM((tm, D), F32)],
        name="experts")(tile_e, tile_valid, xs, w1, w3, w2)


def _combine(y_sorted, slot_pos, w_top, z, mods, gate_idx, *, m, seq, n_batch):
    D = z.shape[1]
    dh = D // 2
    tc = _pick(m, 256, 128)
    n_steps = m // tc

    def body(pos_ref, y_hbm, w_ref, z_ref, g_ref, o_ref, buf_ref, sem):
        i = pl.program_id(0)

        def row_copy(step, slot, r, k):
            src = pos_ref[(step * tc + r) * TOP_K + k]
            return pltpu.make_async_copy(y_hbm.at[pl.ds(src, 1), :],
                                         buf_ref.at[slot, k, pl.ds(r, 1), :], sem.at[slot])

        def start_gather(step, slot):
            def issue(r, carry):
                for k in range(TOP_K):
                    row_copy(step, slot, r, k).start()
                return carry
            lax.fori_loop(0, tc, issue, 0)

        def wait_gather(step, slot):
            def drain(r, carry):
                for k in range(TOP_K):
                    row_copy(step, slot, r, k).wait()
                return carry
            lax.fori_loop(0, tc, drain, 0)

        slot = i % 2

        @pl.when(i == 0)
        def _():
            start_gather(0, 0)

        @pl.when(i + 1 < n_steps)
        def _():
            start_gather(i + 1, 1 - slot)

        wait_gather(i, slot)
        wv = w_ref[...]
        y_lo = jnp.zeros((tc, dh), F32)
        y_hi = jnp.zeros((tc, dh), F32)
        for k in range(TOP_K):
            lo, hi = _unpack_halves(buf_ref[slot, k])
            y_lo = y_lo + wv[:, k:k + 1] * lo
            y_hi = y_hi + wv[:, k:k + 1] * hi
        o_ref[:, :dh] = z_ref[:, :dh] + g_ref[:, :dh] * y_lo
        o_ref[:, dh:] = z_ref[:, dh:] + g_ref[:, dh:] * y_hi

    return _pcall(
        body, grid=(n_steps,), num_scalar_prefetch=1,
        in_specs=[pl.BlockSpec(memory_space=pl.ANY),
                  pl.BlockSpec((tc, TOP_K), lambda i, pos: (i, 0)),
                  pl.BlockSpec((tc, D), lambda i, pos: (i, 0)),
                  _mod_spec(gate_idx, tc, D, seq, n_batch)],
        out_specs=pl.BlockSpec((tc, D), lambda i, pos: (i, 0)),
        out_shape=jax.ShapeDtypeStruct((m, D), F32),
        scratch_shapes=[pltpu.VMEM((2, TOP_K, tc, dh), U32), pltpu.SemaphoreType.DMA((2,))],
        name="moe_combine")(slot_pos, y_sorted, w_top, z, mods)


def _rope_tables(seq, head_dim, n_batch, n_ctx_rows):
    half = head_dim // 2
    inv = ROPE_THETA ** (-jnp.arange(0, half, 2, dtype=F32) / half)
    t = jnp.arange(seq)
    row_ang = (t // GRID_W).astype(F32)[:, None] * inv
    col_ang = (t % GRID_W).astype(F32)[:, None] * inv
    cr, sr, cc, sc = jnp.cos(row_ang), jnp.sin(row_ang), jnp.cos(col_ang), jnp.sin(col_ang)
    cos = jnp.concatenate([cr, cr, cc, cc], axis=-1)
    sin = jnp.concatenate([-sr, sr, -sc, sc], axis=-1)
    cos = jnp.concatenate([jnp.tile(cos, (n_batch, 1)), jnp.ones((n_ctx_rows, head_dim), F32)])
    sin = jnp.concatenate([jnp.tile(sin, (n_batch, 1)), jnp.zeros((n_ctx_rows, head_dim), F32)])
    return cos, sin


def kernel(x, c, ctx, c_ctx, w_mod, b_mod, norm1_g, norm2_g, w_in, b_gate, q_gain, k_gain,
           w_dw, b_dw, conv_ln_g, conv_ln_b, w_attn_o, w_conv_o, w_four_o, w_out,
           w_grp, b_grp, w_rt, b_rt, w1, w3, w2, final_g):
    n_batch, seq, D = x.shape
    ctx_len = ctx.shape[1]
    depth = w_mod.shape[0]
    hd = q_gain.shape[-1]
    q_w = w_attn_o.shape[1]
    conv_w = w_conv_o.shape[1]
    four_w = w_four_o.shape[1]
    kv_w = (w_in.shape[2] - q_w - 2 * conv_w - four_w - 3 * D) // 2
    n_kv = kv_w // hd
    grp = q_w // kv_w
    n_groups = w_grp.shape[2]
    n_experts = w_rt.shape[2]
    per_group = n_experts // n_groups
    o_k, o_v, o_a = q_w, q_w + kv_w, q_w + 2 * kv_w
    o_b, o_f, o_g = o_a + conv_w, o_a + 2 * conv_w, o_a + 2 * conv_w + four_w

    m_lat = n_batch * seq
    m_ctx = n_batch * ctx_len
    m_all = m_lat + m_ctx
    tm = _pick(m_ctx, 512, 256, 128)
    assert seq % tm == 0 and m_ctx % tm == 0 and seq % ctx_len == 0 and seq % GRID_W == 0
    n_vec = n_batch + 1
    assert n_vec <= 8

    z = jnp.concatenate([x.reshape(m_lat, D), ctx.reshape(m_ctx, D)], axis=0)
    cs_t = jnp.zeros((D, 8), F32).at[:, :n_batch].set(c.T).at[:, n_batch].set(c_ctx)
    cos_t, sin_t = _rope_tables(seq, hd, n_batch, m_ctx)
    cn_lat, sn_lat = _dft_tables(seq)
    cn_ctx, sn_ctx = _dft_tables(ctx_len)
    chan_dft = _channel_dft_matrix(four_w, FOUR_GROUPS)
    gw = four_w // FOUR_GROUPS
    seg = dict(seq=seq, n_batch=n_batch)
    plain = lambda dots: dots[0]

    w_in_b, w_out_b = w_in.astype(BF16), w_out.astype(BF16)
    w_attn_o_b, w_conv_o_b, w_four_o_b = (w_attn_o.astype(BF16), w_conv_o.astype(BF16),
                                          w_four_o.astype(BF16))
    w1_b, w3_b, w2_b = w1.astype(BF16), w3.astype(BF16), w2.astype(BF16)
    b_mod3 = b_mod[:, None, :]

    for l in range(depth):
        last = l == depth - 1
        m = m_lat if last else m_all
        mods = _ada(cs_t, w_mod, b_mod3, l, n_vec)[:n_vec].reshape(n_vec * N_MOD, 1, D)

        h = _norm_mod(z, norm1_g[l][None, :], mods, 0, 1, m=m_all, **seg)

        tn_h = _pick(math.gcd(q_w, kv_w), 512, 256, 128)
        rope = [("rows", cos_t), ("rows", sin_t)]
        q = _matmul([h], [(w_in_b, l, 0, 0)], [("full", q_gain[l][None, :])] + rope,
                    _head_norm_rope_epilogue(hd, hd ** -0.5 * math.log2(math.e)),
                    m=m, n=q_w, tm=tm, tn=tn_h, out_dtype=BF16, name="q_proj")
        k = _matmul([h], [(w_in_b, l, 0, o_k // tn_h)], [("full", k_gain[l][None, :])] + rope,
                    _head_norm_rope_epilogue(hd, 1.0),
                    m=m_all, n=kv_w, tm=tm, tn=tn_h, out_dtype=BF16, name="k_proj")
        v = _matmul([h], [(w_in_b, l, 0, o_v // tn_h)], [], plain,
                    m=m_all, n=kv_w, tm=tm, tn=tn_h, out_dtype=BF16, name="v_proj")
        geo = dict(n_batch=n_batch, seq=seq, ctx_len=ctx_len, n_kv=n_kv, grp=grp, hd=hd)
        att = _attn_latent(q, k, v, **geo)
        if not last:
            att = jnp.concatenate([att, _attn_context(q, k, v, **geo)], axis=0)

        tn_c = _pick(math.gcd(conv_w, four_w, o_a), 512, 256, 128)
        u = _matmul([h], [(w_in_b, l, 0, o_a // tn_c), (w_in_b, l, 0, o_b // tn_c)], [],
                    lambda dots: dots[0] * jax.nn.sigmoid(dots[1]),
                    m=m, n=conv_w, tm=tm, tn=tn_c, out_dtype=F32, name="glu_proj")
        conv = _conv_ln_silu(u, w_dw[l], b_dw[l][None, :], conv_ln_g[l][None, :],
                             conv_ln_b[l][None, :], m=m, seq=seq, ctx_len=ctx_len,
                             n_lat_rows=m_lat)

        f = _matmul([h], [(w_in_b, l, 0, o_f // tn_c)], [], plain,
                    m=m, n=four_w, tm=tm, tn=tn_c, out_dtype=BF16, name="four_proj")
        xcs = _matmul([f], [(chan_dft, 0, 0, 0)], [], plain,
                      m=m, n=2 * four_w, tm=tm, tn=tn_c, out_dtype=BF16, name="channel_dft")
        four = _position_dft(xcs, cn_lat, sn_lat, n=seq, row_blk0=0, n_batch=n_batch,
                             width=four_w, scale=float((seq * gw) ** -0.5))
        if not last:
            four_ctx = _position_dft(xcs, cn_ctx, sn_ctx, n=ctx_len, row_blk0=m_lat // ctx_len,
                                     n_batch=n_batch, width=four_w,
                                     scale=float((ctx_len * gw) ** -0.5))
            four = jnp.concatenate([four, four_ctx], axis=0)

        tn_m = _pick(D, 256, 128)
        d_t, g_t = D // tn_m, o_g // tn_m

        def merge(dots, b0, b1, b2):
            return (jax.nn.sigmoid(dots[0] + b0) * dots[1] + jax.nn.sigmoid(dots[2] + b1) * dots[3]
                    + jax.nn.sigmoid(dots[4] + b2) * dots[5])

        bg = b_gate[l][None, :]
        merged = _matmul(
            [h, att, conv, four],
            [(w_in_b, l, 0, g_t), (w_attn_o_b, l, 1, 0),
             (w_in_b, l, 0, g_t + d_t), (w_conv_o_b, l, 2, 0),
             (w_in_b, l, 0, g_t + 2 * d_t), (w_four_o_b, l, 3, 0)],
            [("col", bg, 0), ("col", bg, d_t), ("col", bg, 2 * d_t)], merge,
            m=m, n=D, tm=tm, tn=tn_m, out_dtype=BF16, name="gated_merge")

        tn_o = _pick(D, 1024, 512, 256, 128)
        z1 = _matmul([merged], [(w_out_b, l, 0, 0)], [("tile", z), ("mod", mods, 2)],
                     lambda dots, zt, g: zt + g * dots[0],
                     m=m, n=D, tm=tm, tn=tn_o, out_dtype=F32, name="out_proj", **seg)

        w_router = jnp.zeros((D, ROUTER_LANES), F32)
        w_router = w_router.at[:, :n_groups].set(w_grp[l]).at[:, n_groups:n_groups + n_experts].set(w_rt[l])
        b_router = jnp.zeros((1, ROUTER_LANES), F32)
        b_router = b_router.at[0, :n_groups].set(b_grp[l]).at[0, n_groups:n_groups + n_experts].set(b_rt[l])
        xp, logits = _norm_mod_router(z1, norm2_g[l][None, :], mods, 3, 4, w_router, b_router,
                                      m=m, **seg)
        eid, w_top = _route(logits, n_groups, per_group)
        tm_e = _pick(m * TOP_K, 512, 256, 128)
        slot_pos, tile_e, tile_valid, n_tiles = _dispatch_plan(eid, n_experts, tm_e)
        xs = _dispatch(xp, slot_pos, n_rows_out=n_tiles * tm_e)
        y_sorted = _experts(xs, w1_b, w3_b, w2_b, l, tile_e, tile_valid, n_tiles=n_tiles, tm=tm_e)
        z = _combine(y_sorted, slot_pos, w_top, z1, mods, 5, m=m, **seg)

    out = _final_norm(z, final_g[None, :], m=m_lat)
    return out.reshape(n_batch, seq, D)
```

```python
import math

import numpy as np
import jax
import jax.numpy as jnp
from jax import lax
from jax.experimental import pallas as pl
from jax.experimental.pallas import tpu as pltpu

F32 = jnp.float32
BF16 = jnp.bfloat16
U32 = jnp.uint32

EPS = 1e-6
GRID_W = 64
ROPE_THETA = 10000.0
TOP_K = 2
FOUR_GROUPS = 4
N_MOD = 6
CONV_HALO = 16
ROUTER_LANES = 128
FF_SPLITS = 4

V7X_VMEM_LIMIT_BYTES = 56 * 1024 * 1024


def _pcall(body, *, grid, in_specs, out_specs, out_shape, scratch_shapes=(), name=None,
           num_scalar_prefetch=0, input_output_aliases=None):
    params = pltpu.CompilerParams(dimension_semantics=("arbitrary",) * len(grid),
                                  vmem_limit_bytes=V7X_VMEM_LIMIT_BYTES)
    aliases = input_output_aliases or {}
    if num_scalar_prefetch:
        grid_spec = pltpu.PrefetchScalarGridSpec(
            num_scalar_prefetch=num_scalar_prefetch, grid=grid, in_specs=in_specs,
            out_specs=out_specs, scratch_shapes=scratch_shapes)
        return pl.pallas_call(body, grid_spec=grid_spec, out_shape=out_shape,
                              compiler_params=params, name=name, input_output_aliases=aliases)
    return pl.pallas_call(body, grid=grid, in_specs=in_specs, out_specs=out_specs,
                          out_shape=out_shape, scratch_shapes=scratch_shapes,
                          compiler_params=params, name=name, input_output_aliases=aliases)


def _pick(n, *cands):
    for c in cands:
        if n % c == 0:
            return c
    return n


def _pack_halves(x):
    half = x.shape[1] // 2
    lo = lax.bitcast_convert_type(x[:, :half].astype(BF16).astype(F32), U32)
    hi = lax.bitcast_convert_type(x[:, half:].astype(BF16).astype(F32), U32)
    return hi | (lo >> 16)


def _unpack_halves(p):
    lo = lax.bitcast_convert_type(p << 16, F32)
    hi = lax.bitcast_convert_type(p & jnp.uint32(0xFFFF0000), F32)
    return lo, hi


def _ada(cs_t, w, b, layer, n_vec):
    _, D, N = w.shape
    tk = _pick(D, 512, 256, 128)
    tn = _pick(N, 2048, 1024, 512, 256, 128)

    def body(c_ref, w_ref, b_ref, o_ref):
        @pl.when(pl.program_id(1) == 0)
        def _():
            o_ref[...] = jnp.broadcast_to(b_ref[...], o_ref.shape)

        c = c_ref[...]
        s = c * jax.nn.sigmoid(c)
        wv = w_ref[...]
        rows = [jnp.sum(s[:, r:r + 1] * wv, axis=0, keepdims=True) for r in range(n_vec)]
        rows.append(jnp.zeros((8 - n_vec, tn), F32))
        o_ref[...] += jnp.concatenate(rows, axis=0)

    return _pcall(
        body, grid=(N // tn, D // tk),
        in_specs=[pl.BlockSpec((tk, 8), lambda j, k: (k, 0)),
                  pl.BlockSpec((None, tk, tn), lambda j, k: (layer, k, j)),
                  pl.BlockSpec((None, 1, tn), lambda j, k: (layer, 0, j))],
        out_specs=pl.BlockSpec((8, tn), lambda j, k: (0, j)),
        out_shape=jax.ShapeDtypeStruct((8, N), F32), name="ada")(cs_t, w, b)


def _seg_of_row(row0, seq, n_batch):
    return jnp.minimum(row0 // seq, n_batch)


def _mod_spec(idx, tr, width, seq, n_batch):
    return pl.BlockSpec((None, 1, width),
                        lambda i, *_: (_seg_of_row(i * tr, seq, n_batch) * N_MOD + idx, 0, 0))


def _norm_mod(x, g, mods, sh_idx, sc_idx, *, m, seq, n_batch):
    D = x.shape[1]
    tr = _pick(m, 256, 128)

    def body(x_ref, g_ref, sh_ref, sc_ref, o_ref):
        xf = x_ref[...]
        y = xf * lax.rsqrt(jnp.mean(xf * xf, axis=-1, keepdims=True) + EPS) * g_ref[...]
        o_ref[...] = (y * (1.0 + sc_ref[...]) + sh_ref[...]).astype(o_ref.dtype)

    return _pcall(
        body, grid=(m // tr,),
        in_specs=[pl.BlockSpec((tr, D), lambda i: (i, 0)),
                  pl.BlockSpec((1, D), lambda i: (0, 0)),
                  _mod_spec(sh_idx, tr, D, seq, n_batch),
                  _mod_spec(sc_idx, tr, D, seq, n_batch)],
        out_specs=pl.BlockSpec((tr, D), lambda i: (i, 0)),
        out_shape=jax.ShapeDtypeStruct((m, D), BF16), name="norm_mod")(x, g, mods, mods)


def _norm_mod_router(x, g, mods, sh_idx, sc_idx, w_router, b_router, *, m, seq, n_batch):
    D = x.shape[1]
    tr = _pick(m, 256, 128)

    def body(x_ref, g_ref, sh_ref, sc_ref, wr_ref, br_ref, o_ref, l_ref):
        xf = x_ref[...]
        y = xf * lax.rsqrt(jnp.mean(xf * xf, axis=-1, keepdims=True) + EPS) * g_ref[...]
        h = y * (1.0 + sc_ref[...]) + sh_ref[...]
        o_ref[...] = _pack_halves(h)
        l_ref[...] = jnp.dot(h, wr_ref[...], precision=lax.Precision.HIGHEST,
                             preferred_element_type=F32) + br_ref[...]

    return _pcall(
        body, grid=(m // tr,),
        in_specs=[pl.BlockSpec((tr, D), lambda i: (i, 0)),
                  pl.BlockSpec((1, D), lambda i: (0, 0)),
                  _mod_spec(sh_idx, tr, D, seq, n_batch),
                  _mod_spec(sc_idx, tr, D, seq, n_batch),
                  pl.BlockSpec((D, ROUTER_LANES), lambda i: (0, 0)),
                  pl.BlockSpec((1, ROUTER_LANES), lambda i: (0, 0))],
        out_specs=[pl.BlockSpec((tr, D // 2), lambda i: (i, 0)),
                   pl.BlockSpec((tr, ROUTER_LANES), lambda i: (i, 0))],
        out_shape=[jax.ShapeDtypeStruct((m, D // 2), U32),
                   jax.ShapeDtypeStruct((m, ROUTER_LANES), F32)],
        name="norm_mod_router")(x, g, mods, mods, w_router, b_router)


def _final_norm(x, g, *, m):
    D = x.shape[1]
    tr = _pick(m, 256, 128)

    def body(x_ref, g_ref, o_ref):
        xf = x_ref[...]
        o_ref[...] = xf * lax.rsqrt(jnp.mean(xf * xf, axis=-1, keepdims=True) + EPS) * g_ref[...]

    return _pcall(
        body, grid=(m // tr,),
        in_specs=[pl.BlockSpec((tr, D), lambda i: (i, 0)),
                  pl.BlockSpec((1, D), lambda i: (0, 0))],
        out_specs=pl.BlockSpec((tr, D), lambda i: (i, 0)),
        out_shape=jax.ShapeDtypeStruct((m, D), F32), name="final_norm")(x, g)


def _matmul(xs, ws, extras, epilogue, *, m, n, tm, tn, out_dtype, name, seq=1, n_batch=0):
    in_specs, args = [], []
    for x in xs:
        in_specs.append(pl.BlockSpec((tm, x.shape[1]), lambda j, i: (i, 0)))
        args.append(x)
    for w, layer, _, off in ws:
        in_specs.append(pl.BlockSpec((None, w.shape[1], tn),
                                     lambda j, i, layer=layer, off=off: (layer, 0, j + off)))
        args.append(w)
    for ex in extras:
        kind, arr = ex[0], ex[1]
        if kind == "col":
            in_specs.append(pl.BlockSpec((1, tn), lambda j, i, off=ex[2]: (0, j + off)))
        elif kind == "tile":
            in_specs.append(pl.BlockSpec((tm, tn), lambda j, i: (i, j)))
        elif kind == "rows":
            in_specs.append(pl.BlockSpec((tm, arr.shape[1]), lambda j, i: (i, 0)))
        elif kind == "full":
            in_specs.append(pl.BlockSpec(arr.shape, lambda j, i: (0, 0)))
        elif kind == "mod":
            in_specs.append(pl.BlockSpec(
                (None, 1, tn),
                lambda j, i, idx=ex[2]: (_seg_of_row(i * tm, seq, n_batch) * N_MOD + idx, 0, j)))
        args.append(arr)
    nx, nw = len(xs), len(ws)

    def body(*refs):
        x_refs, w_refs = refs[:nx], refs[nx:nx + nw]
        e_refs, o_ref = refs[nx + nw:-1], refs[-1]
        xv = [r[...] for r in x_refs]
        dots = [jnp.dot(xv[xi], w_refs[p][...], preferred_element_type=F32)
                for p, (_, _, xi, _) in enumerate(ws)]
        o_ref[...] = epilogue(dots, *[r[...] for r in e_refs]).astype(o_ref.dtype)

    return _pcall(body, grid=(n // tn, m // tm), in_specs=in_specs,
                  out_specs=pl.BlockSpec((tm, tn), lambda j, i: (i, j)),
                  out_shape=jax.ShapeDtypeStruct((m, n), out_dtype), name=name)(*args)


def _head_norm_rope_epilogue(head_dim, scale):
    quarter = head_dim // 4

    def epilogue(dots, gain, cos, sin_signed):
        d = dots[0]
        lane = lax.broadcasted_iota(jnp.int32, (d.shape[0], head_dim), 1)
        first_half = (lane % (2 * quarter)) < quarter
        outs = []
        for h in range(d.shape[1] // head_dim):
            xh = d[:, h * head_dim:(h + 1) * head_dim]
            y = xh * lax.rsqrt(jnp.mean(xh * xh, axis=-1, keepdims=True) + EPS) * gain
            partner = jnp.where(first_half, pltpu.roll(y, head_dim - quarter, 1),
                                pltpu.roll(y, quarter, 1))
            outs.append((y * cos + partner * sin_signed) * scale)
        return jnp.concatenate(outs, axis=-1)

    return epilogue


def _softmax_pv(q, keys, values):
    dn = (((1,), (1,)), ((), ()))
    s = [lax.dot_general(q, k, dn, preferred_element_type=F32) for k in keys]
    m = s[0].max(axis=-1, keepdims=True)
    for si in s[1:]:
        m = jnp.maximum(m, si.max(axis=-1, keepdims=True))
    p = [jnp.exp2(si - m) for si in s]
    l = p[0].sum(axis=-1, keepdims=True)
    for pi in p[1:]:
        l = l + pi.sum(axis=-1, keepdims=True)
    o = jnp.dot(p[0].astype(BF16), values[0], preferred_element_type=F32)
    for pi, v in zip(p[1:], values[1:]):
        o = o + jnp.dot(pi.astype(BF16), v, preferred_element_type=F32)
    return o / l


def _attn_latent(q, k, v, cast_srcs, cast_layer, *, n_batch, seq, ctx_len, n_kv, grp, hd):
    tq = _pick(seq, 256, 128)
    ck = _pick(seq, 256, 128)
    qt = seq // tq
    ctx_blk0 = n_batch * seq // ctx_len
    n_steps = n_batch * n_kv * qt
    n_cast = len(cast_srcs)
    dn = (((1,), (1,)), ((), ()))

    def body(q_ref, kl_ref, vl_ref, kc_ref, vc_ref, *rest):
        src_refs, o_ref, dst_refs = rest[:n_cast], rest[n_cast], rest[n_cast + 1:]
        for s_ref, d_ref in zip(src_refs, dst_refs):
            d_ref[...] = s_ref[...].astype(BF16)

        qs = jnp.concatenate([q_ref[:, g * hd:(g + 1) * hd] for g in range(grp)], axis=0)
        chunks = [(kc_ref, vc_ref, 0, ctx_len)]
        chunks += [(kl_ref, vl_ref, c0, ck) for c0 in range(0, seq, ck)]
        m = l = acc = None
        for kr, vr, c0, n in chunks:
            s = lax.dot_general(qs, kr[c0:c0 + n, :], dn, preferred_element_type=F32)
            mc = s.max(axis=-1, keepdims=True)
            if m is None:
                m_new = mc
                p = jnp.exp2(s - m_new)
                l = p.sum(axis=-1, keepdims=True)
                acc = jnp.dot(p.astype(BF16), vr[c0:c0 + n, :], preferred_element_type=F32)
            else:
                m_new = jnp.maximum(m, mc)
                alpha = jnp.exp2(m - m_new)
                p = jnp.exp2(s - m_new)
                l = alpha * l + p.sum(axis=-1, keepdims=True)
                acc = alpha * acc + jnp.dot(p.astype(BF16), vr[c0:c0 + n, :],
                                            preferred_element_type=F32)
            m = m_new
        o = acc / l
        for g in range(grp):
            o_ref[:, g * hd:(g + 1) * hd] = o[g * tq:(g + 1) * tq, :].astype(o_ref.dtype)

    def step(b, h, i):
        return (b * n_kv + h) * qt + i

    lat_spec = pl.BlockSpec((seq, hd), lambda b, h, i: (b, h))
    ctx_spec = pl.BlockSpec((ctx_len, hd), lambda b, h, i: (ctx_blk0 + b, h))
    cast_in, cast_out, cast_shapes = [], [], []
    for w, rows in cast_srcs:
        assert rows % (16 * n_steps) == 0, (rows, n_steps)
        blk = (rows // n_steps, w.shape[1])
        cast_in.append(pl.BlockSpec(blk, lambda b, h, i: (cast_layer * n_steps + step(b, h, i), 0)))
        cast_out.append(pl.BlockSpec(blk, lambda b, h, i: (step(b, h, i), 0)))
        cast_shapes.append(jax.ShapeDtypeStruct((rows, w.shape[1]), BF16))
    q_spec = pl.BlockSpec((tq, grp * hd), lambda b, h, i: (b * qt + i, h))
    outs = _pcall(
        body, grid=(n_batch, n_kv, qt),
        in_specs=[q_spec, lat_spec, lat_spec, ctx_spec, ctx_spec] + cast_in,
        out_specs=[q_spec] + cast_out,
        out_shape=[jax.ShapeDtypeStruct((n_batch * seq, n_kv * grp * hd), BF16)] + cast_shapes,
        name="attn_latent")(q, k, v, k, v, *[w for w, _ in cast_srcs])
    return outs[0], outs[1:]


def _attn_context(q, k, v, *, n_batch, seq, ctx_len, n_kv, grp, hd):
    ctx_blk0 = n_batch * seq // ctx_len

    def body(q_ref, kc_ref, vc_ref, o_ref):
        keys, values = (kc_ref[...],), (vc_ref[...],)
        for g in range(grp):
            o = _softmax_pv(q_ref[:, g * hd:(g + 1) * hd], keys, values)
            o_ref[:, g * hd:(g + 1) * hd] = o.astype(o_ref.dtype)

    ctx_spec = pl.BlockSpec((ctx_len, hd), lambda b, h: (ctx_blk0 + b, h))
    return _pcall(
        body, grid=(n_batch, n_kv),
        in_specs=[pl.BlockSpec((ctx_len, grp * hd), lambda b, h: (ctx_blk0 + b, h)),
                  ctx_spec, ctx_spec],
        out_specs=pl.BlockSpec((ctx_len, grp * hd), lambda b, h: (b, h)),
        out_shape=jax.ShapeDtypeStruct((n_batch * ctx_len, n_kv * grp * hd), BF16),
        name="attn_context")(q, k, v)


def _conv_ln_silu(u, w_dw, b_dw, ln_g, ln_b, *, m, seq, ctx_len, n_lat_rows):
    cw = u.shape[1]
    taps = w_dw.shape[0]
    half = taps // 2
    ts = ctx_len
    halo_per_tile = ts // CONV_HALO
    n_halo_blocks = m // CONV_HALO
    lat_tiles = n_lat_rows // ts
    tiles_per_seq = seq // ts
    rc = 16
    conv_rows = _pick(ts, 128, 64, 32)
    conv_cols = _pick(cw, 128)
    assert CONV_HALO % 8 == 0 and half <= CONV_HALO

    def body(u_ref, prev_ref, next_ref, w_ref, b_ref, g_ref, beta_ref, o_ref, win_ref, y_ref):
        t = pl.program_id(0)
        in_lat = t < lat_tiles
        pos = t % tiles_per_seq
        is_first = jnp.logical_or(jnp.logical_not(in_lat), pos == 0)
        is_last = jnp.logical_or(jnp.logical_not(in_lat), pos == tiles_per_seq - 1)
        win_ref[0:CONV_HALO, :] = jnp.where(is_first, 0.0, prev_ref[...])
        win_ref[CONV_HALO:CONV_HALO + ts, :] = u_ref[...]
        win_ref[CONV_HALO + ts:, :] = jnp.where(is_last, 0.0, next_ref[...])
        for c0 in range(0, ts, conv_rows):
            for j0 in range(0, cw, conv_cols):
                acc = None
                for rho in range(8):
                    part = None
                    n_rows = conv_rows if rho == 0 else conv_rows + 8
                    for kk in range(taps):
                        d = CONV_HALO - half + kk
                        if d % 8 != rho:
                            continue
                        r = c0 + d - rho
                        term = win_ref[r:r + n_rows, j0:j0 + conv_cols] * w_ref[kk:kk + 1, j0:j0 + conv_cols]
                        part = term if part is None else part + term
                    if part is None:
                        continue
                    part = part[rho:rho + conv_rows, :]
                    acc = part if acc is None else acc + part
                y_ref[c0:c0 + conv_rows, j0:j0 + conv_cols] = acc + b_ref[:, j0:j0 + conv_cols]
        for c0 in range(0, ts, rc):
            y = y_ref[c0:c0 + rc, :]
            mu = jnp.mean(y, axis=-1, keepdims=True)
            yc = y - mu
            var = jnp.mean(yc * yc, axis=-1, keepdims=True)
            z = yc * lax.rsqrt(var + EPS) * g_ref[...] + beta_ref[...]
            o_ref[c0:c0 + rc, :] = (z * jax.nn.sigmoid(z)).astype(o_ref.dtype)

    vec = pl.BlockSpec((1, cw), lambda t: (0, 0))
    return _pcall(
        body, grid=(m // ts,),
        in_specs=[pl.BlockSpec((ts, cw), lambda t: (t, 0)),
                  pl.BlockSpec((CONV_HALO, cw),
                               lambda t: (jnp.maximum(t * halo_per_tile - 1, 0), 0)),
                  pl.BlockSpec((CONV_HALO, cw),
                               lambda t: (jnp.minimum((t + 1) * halo_per_tile, n_halo_blocks - 1), 0)),
                  pl.BlockSpec((taps, cw), lambda t: (0, 0)), vec, vec, vec],
        out_specs=pl.BlockSpec((ts, cw), lambda t: (t, 0)),
        out_shape=jax.ShapeDtypeStruct((m, cw), BF16),
        scratch_shapes=[pltpu.VMEM((ts + 2 * CONV_HALO, cw), F32), pltpu.VMEM((ts, cw), F32)],
        name="conv_ln_silu")(u, u, u, w_dw, b_dw, ln_g, ln_b)


def _dft_tables(n):
    j = np.arange(n, dtype=np.int64)[:, None]
    if n % 64 or n <= 64:
        ang = 2.0 * np.pi * ((j * j.T) % n) / n
        return jnp.asarray(np.cos(ang), BF16), jnp.asarray(np.sin(ang), BF16)
    lo = 64
    hi = n // lo
    a_hi = 2.0 * np.pi * ((j * lo * np.arange(hi, dtype=np.int64)[None, :]) % n) / n
    a_lo = 2.0 * np.pi * ((j * np.arange(lo, dtype=np.int64)[None, :]) % n) / n
    ch, sh = jnp.asarray(np.cos(a_hi), F32)[:, :, None], jnp.asarray(np.sin(a_hi), F32)[:, :, None]
    cl, sl = jnp.asarray(np.cos(a_lo), F32)[:, None, :], jnp.asarray(np.sin(a_lo), F32)[:, None, :]
    c = (ch * cl - sh * sl).reshape(n, n).astype(BF16)
    s = (sh * cl + ch * sl).reshape(n, n).astype(BF16)
    return c, s


def _channel_dft_matrix(width, groups):
    gw = width // groups
    k = np.arange(gw, dtype=np.int64)
    ang = 2.0 * np.pi * ((k[:, None] * k[None, :]) % gw) / gw
    out = np.zeros((width, 2 * width), np.float32)
    for g in range(groups):
        sl = slice(g * gw, (g + 1) * gw)
        out[sl, sl] = np.cos(ang)
        out[sl, width + g * gw: width + (g + 1) * gw] = -np.sin(ang)
    return jnp.asarray(out, BF16)[None]


def _position_dft(xcs, cn, sn, *, n, row_blk0, n_batch, width, scale):
    tm = _pick(n, 512, 256, 128)
    tn = _pick(width, 512, 256, 128)
    mt = n // tm
    im_off = width // tn

    def body(c_ref, s_ref, re_ref, im_ref, o_ref):
        acc = jnp.dot(c_ref[...], re_ref[...], preferred_element_type=F32)
        acc = acc + jnp.dot(s_ref[...], im_ref[...], preferred_element_type=F32)
        o_ref[...] = (acc * scale).astype(o_ref.dtype)

    mat_spec = pl.BlockSpec((tm, n), lambda b, j, i: (i, 0))
    return _pcall(
        body, grid=(n_batch, width // tn, mt),
        in_specs=[mat_spec, mat_spec,
                  pl.BlockSpec((n, tn), lambda b, j, i: (row_blk0 + b, j)),
                  pl.BlockSpec((n, tn), lambda b, j, i: (row_blk0 + b, j + im_off))],
        out_specs=pl.BlockSpec((tm, tn), lambda b, j, i: (b * mt + i, j)),
        out_shape=jax.ShapeDtypeStruct((n_batch * n, width), BF16),
        name="position_dft")(cn, sn, xcs, xcs)


def _route(logits, n_groups, per_group):
    glog = logits[:, :n_groups]
    g_hot = jax.nn.one_hot(jnp.argmax(glog, axis=-1), n_groups, dtype=F32)
    p_grp = jnp.sum(jax.nn.softmax(glog, axis=-1) * g_hot, axis=-1, keepdims=True)
    elog = logits[:, n_groups:n_groups + n_groups * per_group].reshape(-1, n_groups, per_group)
    elog_sel = jnp.sum(elog * g_hot[:, :, None], axis=1)
    i0 = jnp.argmax(elog_sel, axis=-1)
    hot0 = jax.nn.one_hot(i0, per_group, dtype=jnp.bool_)
    rest = jnp.where(hot0, -jnp.inf, elog_sel)
    i1 = jnp.argmax(rest, axis=-1)
    top_v = jnp.stack([jnp.max(elog_sel, axis=-1), jnp.max(rest, axis=-1)], axis=-1)
    w_top = jax.nn.softmax(top_v, axis=-1) * p_grp
    gsel = jnp.argmax(glog, axis=-1)
    eid = gsel[:, None] * per_group + jnp.stack([i0, i1], axis=-1)
    return eid.astype(jnp.int32), w_top


def _dispatch_plan(eid, n_experts, tm):
    n_slots = eid.shape[0] * TOP_K
    n_tiles = n_slots // tm + n_experts
    hot = (eid.reshape(-1)[:, None] == jnp.arange(n_experts, dtype=jnp.int32)[None, :])
    hot = hot.astype(jnp.int32)
    running = jnp.cumsum(hot, axis=0)
    counts = running[-1]
    padded = ((counts + tm - 1) // tm) * tm
    pad_end = jnp.cumsum(padded)
    pad_start = pad_end - padded
    slot_pos = jnp.sum(hot * (running - 1 + pad_start[None, :]), axis=1).astype(jnp.int32)
    tile_row0 = jnp.arange(n_tiles, dtype=jnp.int32) * tm
    tile_e = jnp.sum((tile_row0[:, None] >= pad_end[None, :]).astype(jnp.int32), axis=1)
    tile_e = jnp.minimum(tile_e, n_experts - 1).astype(jnp.int32)
    tile_valid = (tile_row0 < pad_end[-1]).astype(jnp.int32)
    return slot_pos, tile_e, tile_valid, n_tiles


def _dispatch(xp, slot_pos, *, n_rows_out):
    m, dh = xp.shape
    tr = _pick(m, 256, 128)

    def body(pos_ref, x_ref, init_hbm, o_hbm, sem):
        del init_hbm
        i = pl.program_id(0)

        def row_copy(r, k):
            dst = pos_ref[(i * tr + r) * TOP_K + k]
            return pltpu.make_async_copy(x_ref.at[pl.ds(r, 1), :], o_hbm.at[pl.ds(dst, 1), :],
                                         sem.at[0])

        def issue(r, carry):
            for k in range(TOP_K):
                row_copy(r, k).start()
            return carry

        lax.fori_loop(0, tr, issue, 0, unroll=8)
        for k in range(TOP_K):
            pltpu.make_async_copy(x_ref, o_hbm.at[pl.ds(0, tr), :], sem.at[0]).wait()

    return _pcall(
        body, grid=(m // tr,), num_scalar_prefetch=1,
        in_specs=[pl.BlockSpec((tr, dh), lambda i, pos: (i, 0)),
                  pl.BlockSpec(memory_space=pl.ANY)],
        out_specs=pl.BlockSpec(memory_space=pl.ANY),
        out_shape=jax.ShapeDtypeStruct((n_rows_out, dh), U32),
        scratch_shapes=[pltpu.SemaphoreType.DMA((1,))],
        input_output_aliases={2: 0},
        name="moe_dispatch")(slot_pos, xp, jnp.zeros((n_rows_out, dh), U32))


def _experts(xs, w1, w3, w2, tile_e, tile_valid, *, n_tiles, tm):
    dh = xs.shape[1]
    D = 2 * dh
    fq = w1.shape[2] // FF_SPLITS
    cn = _pick(D, 512, 256, 128)

    def blk(i, q):
        return jnp.where(i % 2 == 0, q, FF_SPLITS - 1 - q)

    def body(te_ref, tv_ref, x_ref, w1_ref, w3_ref, w2_ref, o_ref, xlo_ref, xhi_ref, acc_ref):
        i, q = pl.program_id(0), pl.program_id(1)
        valid = tv_ref[i] == 1

        @pl.when(jnp.logical_and(valid, q == 0))
        def _():
            lo, hi = _unpack_halves(x_ref[...])
            xlo_ref[...] = lo.astype(BF16)
            xhi_ref[...] = hi.astype(BF16)

        @pl.when(valid)
        def _():
            xlo, xhi = xlo_ref[...], xhi_ref[...]
            a = (jnp.dot(xlo, w1_ref[:dh, :], preferred_element_type=F32)
                 + jnp.dot(xhi, w1_ref[dh:, :], preferred_element_type=F32))
            b = (jnp.dot(xlo, w3_ref[:dh, :], preferred_element_type=F32)
                 + jnp.dot(xhi, w3_ref[dh:, :], preferred_element_type=F32))
            hid = (a * jax.nn.sigmoid(a) * b).astype(BF16)

            @pl.when(q == 0)
            def _():
                for c in range(0, D, cn):
                    acc_ref[:, c:c + cn] = jnp.dot(hid, w2_ref[:, c:c + cn],
                                                   preferred_element_type=F32)

            @pl.when(q > 0)
            def _():
                for c in range(0, D, cn):
                    acc_ref[:, c:c + cn] += jnp.dot(hid, w2_ref[:, c:c + cn],
                                                    preferred_element_type=F32)

        @pl.when(q == FF_SPLITS - 1)
        def _():
            @pl.when(valid)
            def _():
                o_ref[...] = _pack_halves(acc_ref[...])

            @pl.when(jnp.logical_not(valid))
            def _():
                o_ref[...] = jnp.zeros(o_ref.shape, o_ref.dtype)

    return _pcall(
        body, grid=(n_tiles, FF_SPLITS), num_scalar_prefetch=2,
        in_specs=[pl.BlockSpec((tm, dh), lambda i, q, te, tv: (i, 0)),
                  pl.BlockSpec((None, D, fq), lambda i, q, te, tv: (te[i], 0, blk(i, q))),
                  pl.BlockSpec((None, D, fq), lambda i, q, te, tv: (te[i], 0, blk(i, q))),
                  pl.BlockSpec((None, fq, D), lambda i, q, te, tv: (te[i], blk(i, q), 0))],
        out_specs=pl.BlockSpec((tm, dh), lambda i, q, te, tv: (i, 0)),
        out_shape=jax.ShapeDtypeStruct((n_tiles * tm, dh), U32),
        scratch_shapes=[pltpu.VMEM((tm, dh), BF16), pltpu.VMEM((tm, dh), BF16),
                        pltpu.VMEM((tm, D), F32)],
        name="experts")(tile_e, tile_valid, xs, w1, w3, w2)


def _combine(y_sorted, slot_pos, w_top, z, mods, gate_idx, *, m, seq, n_batch):
    D = z.shape[1]
    dh = D // 2
    tc = _pick(m, 256, 128)
    n_steps = m // tc

    def body(pos_ref, y_hbm, w_ref, z_ref, g_ref, o_ref, buf_ref, sem):
        i = pl.program_id(0)

        def row_copy(step, slot, r, k):
            src = pos_ref[(step * tc + r) * TOP_K + k]
            return pltpu.make_async_copy(y_hbm.at[pl.ds(src, 1), :],
                                         buf_ref.at[slot, k, pl.ds(r, 1), :], sem.at[slot])

        def start_gather(step, slot):
            def issue(r, carry):
                for k in range(TOP_K):
                    row_copy(step, slot, r, k).start()
                return carry
            lax.fori_loop(0, tc, issue, 0, unroll=8)

        def wait_gather(slot):
            for k in range(TOP_K):
                pltpu.make_async_copy(y_hbm.at[pl.ds(0, tc), :], buf_ref.at[slot, k],
                                      sem.at[slot]).wait()

        slot = i % 2

        @pl.when(i == 0)
        def _():
            start_gather(0, 0)

        @pl.when(i + 1 < n_steps)
        def _():
            start_gather(i + 1, 1 - slot)

        wait_gather(slot)
        wv = w_ref[...]
        y_lo = jnp.zeros((tc, dh), F32)
        y_hi = jnp.zeros((tc, dh), F32)
        for k in range(TOP_K):
            lo, hi = _unpack_halves(buf_ref[slot, k])
            y_lo = y_lo + wv[:, k:k + 1] * lo
            y_hi = y_hi + wv[:, k:k + 1] * hi
        o_ref[:, :dh] = z_ref[:, :dh] + g_ref[:, :dh] * y_lo
        o_ref[:, dh:] = z_ref[:, dh:] + g_ref[:, dh:] * y_hi

    return _pcall(
        body, grid=(n_steps,), num_scalar_prefetch=1,
        in_specs=[pl.BlockSpec(memory_space=pl.ANY),
                  pl.BlockSpec((tc, TOP_K), lambda i, pos: (i, 0)),
                  pl.BlockSpec((tc, D), lambda i, pos: (i, 0)),
                  _mod_spec(gate_idx, tc, D, seq, n_batch)],
        out_specs=pl.BlockSpec((tc, D), lambda i, pos: (i, 0)),
        out_shape=jax.ShapeDtypeStruct((m, D), F32),
        scratch_shapes=[pltpu.VMEM((2, TOP_K, tc, dh), U32), pltpu.SemaphoreType.DMA((2,))],
        name="moe_combine")(slot_pos, y_sorted, w_top, z, mods)


def _rope_tables(seq, head_dim, n_batch, n_ctx_rows):
    half = head_dim // 2
    inv = ROPE_THETA ** (-jnp.arange(0, half, 2, dtype=F32) / half)
    t = jnp.arange(seq)
    row_ang = (t // GRID_W).astype(F32)[:, None] * inv
    col_ang = (t % GRID_W).astype(F32)[:, None] * inv
    cr, sr, cc, sc = jnp.cos(row_ang), jnp.sin(row_ang), jnp.cos(col_ang), jnp.sin(col_ang)
    cos = jnp.concatenate([cr, cr, cc, cc], axis=-1)
    sin = jnp.concatenate([-sr, sr, -sc, sc], axis=-1)
    cos = jnp.concatenate([jnp.tile(cos, (n_batch, 1)), jnp.ones((n_ctx_rows, head_dim), F32)])
    sin = jnp.concatenate([jnp.tile(sin, (n_batch, 1)), jnp.zeros((n_ctx_rows, head_dim), F32)])
    return cos, sin


def kernel(x, c, ctx, c_ctx, w_mod, b_mod, norm1_g, norm2_g, w_in, b_gate, q_gain, k_gain,
           w_dw, b_dw, conv_ln_g, conv_ln_b, w_attn_o, w_conv_o, w_four_o, w_out,
           w_grp, b_grp, w_rt, b_rt, w1, w3, w2, final_g):
    n_batch, seq, D = x.shape
    ctx_len = ctx.shape[1]
    depth = w_mod.shape[0]
    hd = q_gain.shape[-1]
    q_w = w_attn_o.shape[1]
    conv_w = w_conv_o.shape[1]
    four_w = w_four_o.shape[1]
    kv_w = (w_in.shape[2] - q_w - 2 * conv_w - four_w - 3 * D) // 2
    n_kv = kv_w // hd
    grp = q_w // kv_w
    n_groups = w_grp.shape[2]
    n_experts = w_rt.shape[2]
    per_group = n_experts // n_groups
    o_k, o_v, o_a = q_w, q_w + kv_w, q_w + 2 * kv_w
    o_b, o_f, o_g = o_a + conv_w, o_a + 2 * conv_w, o_a + 2 * conv_w + four_w

    m_lat = n_batch * seq
    m_ctx = n_batch * ctx_len
    m_all = m_lat + m_ctx
    tm = _pick(m_ctx, 512, 256, 128)
    assert seq % tm == 0 and m_ctx % tm == 0 and seq % ctx_len == 0 and seq % GRID_W == 0
    n_vec = n_batch + 1
    assert n_vec <= 8

    z = jnp.concatenate([x.reshape(m_lat, D), ctx.reshape(m_ctx, D)], axis=0)
    cs_t = jnp.zeros((D, 8), F32).at[:, :n_batch].set(c.T).at[:, n_batch].set(c_ctx)
    cos_t, sin_t = _rope_tables(seq, hd, n_batch, m_ctx)
    cn_lat, sn_lat = _dft_tables(seq)
    cn_ctx, sn_ctx = _dft_tables(ctx_len)
    chan_dft = _channel_dft_matrix(four_w, FOUR_GROUPS)
    gw = four_w // FOUR_GROUPS
    seg = dict(seq=seq, n_batch=n_batch)
    plain = lambda dots: dots[0]

    w_in_b, w_out_b = w_in.astype(BF16), w_out.astype(BF16)
    w_attn_o_b, w_conv_o_b, w_four_o_b = (w_attn_o.astype(BF16), w_conv_o.astype(BF16),
                                          w_four_o.astype(BF16))
    ff = w1.shape[3]
    expert_srcs = [(w1.reshape(depth * n_experts * D, ff), n_experts * D),
                   (w3.reshape(depth * n_experts * D, ff), n_experts * D),
                   (w2.reshape(depth * n_experts * ff, D), n_experts * ff)]
    b_mod3 = b_mod[:, None, :]

    for l in range(depth):
        last = l == depth - 1
        m = m_lat if last else m_all
        mods = _ada(cs_t, w_mod, b_mod3, l, n_vec)[:n_vec].reshape(n_vec * N_MOD, 1, D)

        h = _norm_mod(z, norm1_g[l][None, :], mods, 0, 1, m=m_all, **seg)

        tn_h = _pick(math.gcd(q_w, kv_w), 512, 256, 128)
        rope = [("rows", cos_t), ("rows", sin_t)]
        q = _matmul([h], [(w_in_b, l, 0, 0)], [("full", q_gain[l][None, :])] + rope,
                    _head_norm_rope_epilogue(hd, hd ** -0.5 * math.log2(math.e)),
                    m=m, n=q_w, tm=tm, tn=tn_h, out_dtype=BF16, name="q_proj")
        k = _matmul([h], [(w_in_b, l, 0, o_k // tn_h)], [("full", k_gain[l][None, :])] + rope,
                    _head_norm_rope_epilogue(hd, 1.0),
                    m=m_all, n=kv_w, tm=tm, tn=tn_h, out_dtype=BF16, name="k_proj")
        v = _matmul([h], [(w_in_b, l, 0, o_v // tn_h)], [], plain,
                    m=m_all, n=kv_w, tm=tm, tn=tn_h, out_dtype=BF16, name="v_proj")
        geo = dict(n_batch=n_batch, seq=seq, ctx_len=ctx_len, n_kv=n_kv, grp=grp, hd=hd)
        att, (w1_b, w3_b, w2_b) = _attn_latent(q, k, v, expert_srcs, l, **geo)
        w1_b = w1_b.reshape(n_experts, D, ff)
        w3_b = w3_b.reshape(n_experts, D, ff)
        w2_b = w2_b.reshape(n_experts, ff, D)
        if not last:
            att = jnp.concatenate([att, _attn_context(q, k, v, **geo)], axis=0)

        tn_c = _pick(math.gcd(conv_w, four_w, o_a), 512, 256, 128)
        u = _matmul([h], [(w_in_b, l, 0, o_a // tn_c), (w_in_b, l, 0, o_b // tn_c)], [],
                    lambda dots: dots[0] * jax.nn.sigmoid(dots[1]),
                    m=m, n=conv_w, tm=tm, tn=tn_c, out_dtype=F32, name="glu_proj")
        conv = _conv_ln_silu(u, w_dw[l], b_dw[l][None, :], conv_ln_g[l][None, :],
                             conv_ln_b[l][None, :], m=m, seq=seq, ctx_len=ctx_len,
                             n_lat_rows=m_lat)

        f = _matmul([h], [(w_in_b, l, 0, o_f // tn_c)], [], plain,
                    m=m, n=four_w, tm=tm, tn=tn_c, out_dtype=BF16, name="four_proj")
        xcs = _matmul([f], [(chan_dft, 0, 0, 0)], [], plain,
                      m=m, n=2 * four_w, tm=tm, tn=tn_c, out_dtype=BF16, name="channel_dft")
        four = _position_dft(xcs, cn_lat, sn_lat, n=seq, row_blk0=0, n_batch=n_batch,
                             width=four_w, scale=float((seq * gw) ** -0.5))
        if not last:
            four_ctx = _position_dft(xcs, cn_ctx, sn_ctx, n=ctx_len, row_blk0=m_lat // ctx_len,
                                     n_batch=n_batch, width=four_w,
                                     scale=float((ctx_len * gw) ** -0.5))
            four = jnp.concatenate([four, four_ctx], axis=0)

        tn_m = _pick(D, 256, 128)
        d_t, g_t = D // tn_m, o_g // tn_m

        def merge(dots, b0, b1, b2):
            return (jax.nn.sigmoid(dots[0] + b0) * dots[1] + jax.nn.sigmoid(dots[2] + b1) * dots[3]
                    + jax.nn.sigmoid(dots[4] + b2) * dots[5])

        bg = b_gate[l][None, :]
        merged = _matmul(
            [h, att, conv, four],
            [(w_in_b, l, 0, g_t), (w_attn_o_b, l, 1, 0),
             (w_in_b, l, 0, g_t + d_t), (w_conv_o_b, l, 2, 0),
             (w_in_b, l, 0, g_t + 2 * d_t), (w_four_o_b, l, 3, 0)],
            [("col", bg, 0), ("col", bg, d_t), ("col", bg, 2 * d_t)], merge,
            m=m, n=D, tm=tm, tn=tn_m, out_dtype=BF16, name="gated_merge")

        tn_o = _pick(D, 1024, 512, 256, 128)
        z1 = _matmul([merged], [(w_out_b, l, 0, 0)], [("tile", z), ("mod", mods, 2)],
                     lambda dots, zt, g: zt + g * dots[0],
                     m=m, n=D, tm=tm, tn=tn_o, out_dtype=F32, name="out_proj", **seg)

        w_router = jnp.zeros((D, ROUTER_LANES), F32)
        w_router = w_router.at[:, :n_groups].set(w_grp[l]).at[:, n_groups:n_groups + n_experts].set(w_rt[l])
        b_router = jnp.zeros((1, ROUTER_LANES), F32)
        b_router = b_router.at[0, :n_groups].set(b_grp[l]).at[0, n_groups:n_groups + n_experts].set(b_rt[l])
        xp, logits = _norm_mod_router(z1, norm2_g[l][None, :], mods, 3, 4, w_router, b_router,
                                      m=m, **seg)
        eid, w_top = _route(logits, n_groups, per_group)
        tm_e = _pick(m * TOP_K, 512, 256, 128)
        slot_pos, tile_e, tile_valid, n_tiles = _dispatch_plan(eid, n_experts, tm_e)
        xs = _dispatch(xp, slot_pos, n_rows_out=n_tiles * tm_e)
        y_sorted = _experts(xs, w1_b, w3_b, w2_b, tile_e, tile_valid, n_tiles=n_tiles, tm=tm_e)
        z = _combine(y_sorted, slot_pos, w_top, z1, mods, 5, m=m, **seg)

    out = _final_norm(z, final_g[None, :], m=m_lat)
    return out.reshape(n_batch, seq, D)
```

```python
import math

import numpy as np
import jax
import jax.numpy as jnp
from jax import lax
from jax.experimental import pallas as pl
from jax.experimental.pallas import tpu as pltpu

F32 = jnp.float32
BF16 = jnp.bfloat16
U32 = jnp.uint32

EPS = 1e-6
GRID_W = 64
ROPE_THETA = 10000.0
TOP_K = 2
FOUR_GROUPS = 4
N_MOD = 6
CONV_HALO = 16
ROUTER_LANES = 128
FF_SPLITS = 4

V7X_VMEM_LIMIT_BYTES = 56 * 1024 * 1024


def _pcall(body, *, grid, in_specs, out_specs, out_shape, scratch_shapes=(), name=None,
           num_scalar_prefetch=0, input_output_aliases=None):
    params = pltpu.CompilerParams(dimension_semantics=("arbitrary",) * len(grid),
                                  vmem_limit_bytes=V7X_VMEM_LIMIT_BYTES)
    aliases = input_output_aliases or {}
    if num_scalar_prefetch:
        grid_spec = pltpu.PrefetchScalarGridSpec(
            num_scalar_prefetch=num_scalar_prefetch, grid=grid, in_specs=in_specs,
            out_specs=out_specs, scratch_shapes=scratch_shapes)
        return pl.pallas_call(body, grid_spec=grid_spec, out_shape=out_shape,
                              compiler_params=params, name=name, input_output_aliases=aliases)
    return pl.pallas_call(body, grid=grid, in_specs=in_specs, out_specs=out_specs,
                          out_shape=out_shape, scratch_shapes=scratch_shapes,
                          compiler_params=params, name=name, input_output_aliases=aliases)


def _pick(n, *cands):
    for c in cands:
        if n % c == 0:
            return c
    return n


def _pack_halves(x):
    half = x.shape[1] // 2
    lo = lax.bitcast_convert_type(x[:, :half].astype(BF16).astype(F32), U32)
    hi = lax.bitcast_convert_type(x[:, half:].astype(BF16).astype(F32), U32)
    return hi | (lo >> 16)


def _unpack_halves(p):
    lo = lax.bitcast_convert_type(p << 16, F32)
    hi = lax.bitcast_convert_type(p & jnp.uint32(0xFFFF0000), F32)
    return lo, hi


def _ada(cs_t, w, b, layer, n_vec):
    _, D, N = w.shape
    tk = _pick(D, 512, 256, 128)
    tn = _pick(N, 2048, 1024, 512, 256, 128)

    def body(c_ref, w_ref, b_ref, o_ref):
        @pl.when(pl.program_id(1) == 0)
        def _():
            o_ref[...] = jnp.broadcast_to(b_ref[...], o_ref.shape)

        c = c_ref[...]
        s = c * jax.nn.sigmoid(c)
        wv = w_ref[...]
        rows = [jnp.sum(s[:, r:r + 1] * wv, axis=0, keepdims=True) for r in range(n_vec)]
        rows.append(jnp.zeros((8 - n_vec, tn), F32))
        o_ref[...] += jnp.concatenate(rows, axis=0)

    return _pcall(
        body, grid=(N // tn, D // tk),
        in_specs=[pl.BlockSpec((tk, 8), lambda j, k: (k, 0)),
                  pl.BlockSpec((None, tk, tn), lambda j, k: (layer, k, j)),
                  pl.BlockSpec((None, 1, tn), lambda j, k: (layer, 0, j))],
        out_specs=pl.BlockSpec((8, tn), lambda j, k: (0, j)),
        out_shape=jax.ShapeDtypeStruct((8, N), F32), name="ada")(cs_t, w, b)


def _seg_of_row(row0, seq, n_batch):
    return jnp.minimum(row0 // seq, n_batch)


def _mod_spec(idx, tr, width, seq, n_batch):
    return pl.BlockSpec((None, 1, width),
                        lambda i, *_: (_seg_of_row(i * tr, seq, n_batch) * N_MOD + idx, 0, 0))


def _norm_mod(x, x_tail, g, mods, sh_idx, sc_idx, *, m, seq, n_batch):
    D = x.shape[1]
    tr = _pick(m, 256, 128)
    head_tiles = x.shape[0] // tr
    if x_tail is None:
        x_tail = x

    def body(x_ref, t_ref, g_ref, sh_ref, sc_ref, o_ref):
        xf = jnp.where(pl.program_id(0) < head_tiles, x_ref[...], t_ref[...])
        y = xf * lax.rsqrt(jnp.mean(xf * xf, axis=-1, keepdims=True) + EPS) * g_ref[...]
        o_ref[...] = (y * (1.0 + sc_ref[...]) + sh_ref[...]).astype(o_ref.dtype)

    return _pcall(
        body, grid=(m // tr,),
        in_specs=[pl.BlockSpec((tr, D), lambda i: (jnp.minimum(i, head_tiles - 1), 0)),
                  pl.BlockSpec((tr, D), lambda i: (jnp.maximum(i - head_tiles, 0), 0)),
                  pl.BlockSpec((1, D), lambda i: (0, 0)),
                  _mod_spec(sh_idx, tr, D, seq, n_batch),
                  _mod_spec(sc_idx, tr, D, seq, n_batch)],
        out_specs=pl.BlockSpec((tr, D), lambda i: (i, 0)),
        out_shape=jax.ShapeDtypeStruct((m, D), BF16), name="norm_mod")(x, x_tail, g, mods, mods)


def _norm_mod_router(x, g, mods, sh_idx, sc_idx, w_router, b_router, *, m, seq, n_batch):
    D = x.shape[1]
    tr = _pick(m, 256, 128)

    def body(x_ref, g_ref, sh_ref, sc_ref, wr_ref, br_ref, o_ref, l_ref):
        xf = x_ref[...]
        y = xf * lax.rsqrt(jnp.mean(xf * xf, axis=-1, keepdims=True) + EPS) * g_ref[...]
        h = y * (1.0 + sc_ref[...]) + sh_ref[...]
        o_ref[...] = _pack_halves(h)
        l_ref[...] = jnp.dot(h, wr_ref[...], precision=lax.Precision.HIGHEST,
                             preferred_element_type=F32) + br_ref[...]

    return _pcall(
        body, grid=(m // tr,),
        in_specs=[pl.BlockSpec((tr, D), lambda i: (i, 0)),
                  pl.BlockSpec((1, D), lambda i: (0, 0)),
                  _mod_spec(sh_idx, tr, D, seq, n_batch),
                  _mod_spec(sc_idx, tr, D, seq, n_batch),
                  pl.BlockSpec((D, ROUTER_LANES), lambda i: (0, 0)),
                  pl.BlockSpec((1, ROUTER_LANES), lambda i: (0, 0))],
        out_specs=[pl.BlockSpec((tr, D // 2), lambda i: (i, 0)),
                   pl.BlockSpec((tr, ROUTER_LANES), lambda i: (i, 0))],
        out_shape=[jax.ShapeDtypeStruct((m, D // 2), U32),
                   jax.ShapeDtypeStruct((m, ROUTER_LANES), F32)],
        name="norm_mod_router")(x, g, mods, mods, w_router, b_router)


def _matmul(xs, ws, extras, epilogue, *, m, n, tm, tn, out_dtype, name, seq=1, n_batch=0,
            row_split=1):
    in_specs, args = [], []
    for x in xs:
        in_specs.append(pl.BlockSpec((tm, x.shape[1]), lambda j, i: (i, 0)))
        args.append(x)
    for w, layer, _, off in ws:
        in_specs.append(pl.BlockSpec((None, w.shape[1], tn),
                                     lambda j, i, layer=layer, off=off: (layer, 0, j + off)))
        args.append(w)
    for ex in extras:
        kind, arr = ex[0], ex[1]
        if kind == "col":
            in_specs.append(pl.BlockSpec((1, tn), lambda j, i, off=ex[2]: (0, j + off)))
        elif kind == "tile":
            in_specs.append(pl.BlockSpec((tm, tn), lambda j, i: (i, j)))
        elif kind == "tile2":
            head_tiles = arr.shape[0] // tm
            in_specs.append(pl.BlockSpec(
                (tm, tn), lambda j, i, ht=head_tiles: (jnp.minimum(i, ht - 1), j)))
            args.append(arr)
            in_specs.append(pl.BlockSpec(
                (tm, tn), lambda j, i, ht=head_tiles: (jnp.maximum(i - ht, 0), j)))
            arr = ex[2]
        elif kind == "rows":
            in_specs.append(pl.BlockSpec((tm, arr.shape[1]), lambda j, i: (i, 0)))
        elif kind == "full":
            in_specs.append(pl.BlockSpec(arr.shape, lambda j, i: (0, 0)))
        elif kind == "mod":
            in_specs.append(pl.BlockSpec(
                (None, 1, tn),
                lambda j, i, idx=ex[2]: (_seg_of_row(i * tm, seq, n_batch) * N_MOD + idx, 0, j)))
        args.append(arr)
    nx, nw = len(xs), len(ws)

    sub = tm // row_split

    def body(*refs):
        x_refs, w_refs = refs[:nx], refs[nx:nx + nw]
        e_refs, o_ref = list(refs[nx + nw:-1]), refs[-1]
        i = pl.program_id(1)
        for r0 in range(0, tm, sub):
            rows = slice(r0, r0 + sub)
            dots = [jnp.dot(x_refs[xi][rows, :], w_refs[p][...], preferred_element_type=F32)
                    for p, (_, _, xi, _) in enumerate(ws)]
            vals, e = [], 0
            for ex in extras:
                if ex[0] == "tile2":
                    in_head = i < ex[1].shape[0] // tm
                    vals.append(jnp.where(in_head, e_refs[e][rows, :], e_refs[e + 1][rows, :]))
                    e += 2
                    continue
                vals.append(e_refs[e][rows, :] if ex[0] in ("tile", "rows") else e_refs[e][...])
                e += 1
            o_ref[rows, :] = epilogue(dots, *vals).astype(o_ref.dtype)

    return _pcall(body, grid=(n // tn, m // tm), in_specs=in_specs,
                  out_specs=pl.BlockSpec((tm, tn), lambda j, i: (i, j)),
                  out_shape=jax.ShapeDtypeStruct((m, n), out_dtype), name=name)(*args)


def _head_norm_rope_epilogue(head_dim, scale):
    quarter = head_dim // 4

    def epilogue(dots, gain, cos, sin_signed):
        d = dots[0]
        lane = lax.broadcasted_iota(jnp.int32, (d.shape[0], head_dim), 1)
        first_half = (lane % (2 * quarter)) < quarter
        outs = []
        for h in range(d.shape[1] // head_dim):
            xh = d[:, h * head_dim:(h + 1) * head_dim]
            y = xh * lax.rsqrt(jnp.mean(xh * xh, axis=-1, keepdims=True) + EPS) * gain
            partner = jnp.where(first_half, pltpu.roll(y, head_dim - quarter, 1),
                                pltpu.roll(y, quarter, 1))
            outs.append((y * cos + partner * sin_signed) * scale)
        return jnp.concatenate(outs, axis=-1)

    return epilogue


def _softmax_pv(q, keys, values):
    dn = (((1,), (1,)), ((), ()))
    s = [lax.dot_general(q, k, dn, preferred_element_type=F32) for k in keys]
    m = s[0].max(axis=-1, keepdims=True)
    for si in s[1:]:
        m = jnp.maximum(m, si.max(axis=-1, keepdims=True))
    p = [jnp.exp2(si - m) for si in s]
    l = p[0].sum(axis=-1, keepdims=True)
    for pi in p[1:]:
        l = l + pi.sum(axis=-1, keepdims=True)
    o = jnp.dot(p[0].astype(BF16), values[0], preferred_element_type=F32)
    for pi, v in zip(p[1:], values[1:]):
        o = o + jnp.dot(pi.astype(BF16), v, preferred_element_type=F32)
    return o / l


def _attn_latent(q, k, v, cast_srcs, cast_layer, *, n_batch, seq, ctx_len, n_kv, grp, hd):
    tq = _pick(seq, 256, 128)
    ck = _pick(seq, 256, 128)
    qt = seq // tq
    ctx_blk0 = n_batch * seq // ctx_len
    n_steps = n_batch * n_kv * qt
    n_cast = len(cast_srcs)
    dn = (((1,), (1,)), ((), ()))

    def body(q_ref, kl_ref, vl_ref, kc_ref, vc_ref, *rest):
        src_refs, o_ref, dst_refs = rest[:n_cast], rest[n_cast], rest[n_cast + 1:]
        for s_ref, d_ref in zip(src_refs, dst_refs):
            d_ref[...] = s_ref[...].astype(BF16)

        qs = jnp.concatenate([q_ref[:, g * hd:(g + 1) * hd] for g in range(grp)], axis=0)
        chunks = [(kc_ref, vc_ref, 0, ctx_len)]
        chunks += [(kl_ref, vl_ref, c0, ck) for c0 in range(0, seq, ck)]
        m = l = acc = None
        for kr, vr, c0, n in chunks:
            s = lax.dot_general(qs, kr[c0:c0 + n, :], dn, preferred_element_type=F32)
            mc = s.max(axis=-1, keepdims=True)
            if m is None:
                m_new = mc
                p = jnp.exp2(s - m_new)
                l = p.sum(axis=-1, keepdims=True)
                acc = jnp.dot(p.astype(BF16), vr[c0:c0 + n, :], preferred_element_type=F32)
            else:
                m_new = jnp.maximum(m, mc)
                alpha = jnp.exp2(m - m_new)
                p = jnp.exp2(s - m_new)
                l = alpha * l + p.sum(axis=-1, keepdims=True)
                acc = alpha * acc + jnp.dot(p.astype(BF16), vr[c0:c0 + n, :],
                                            preferred_element_type=F32)
            m = m_new
        o = acc / l
        for g in range(grp):
            o_ref[:, g * hd:(g + 1) * hd] = o[g * tq:(g + 1) * tq, :].astype(o_ref.dtype)

    def step(b, h, i):
        return (b * n_kv + h) * qt + i

    lat_spec = pl.BlockSpec((seq, hd), lambda b, h, i: (b, h))
    ctx_spec = pl.BlockSpec((ctx_len, hd), lambda b, h, i: (ctx_blk0 + b, h))
    cast_in, cast_out, cast_shapes = [], [], []
    for w, rows in cast_srcs:
        assert rows % (16 * n_steps) == 0, (rows, n_steps)
        blk = (rows // n_steps, w.shape[1])
        cast_in.append(pl.BlockSpec(blk, lambda b, h, i: (cast_layer * n_steps + step(b, h, i), 0)))
        cast_out.append(pl.BlockSpec(blk, lambda b, h, i: (step(b, h, i), 0)))
        cast_shapes.append(jax.ShapeDtypeStruct((rows, w.shape[1]), BF16))
    q_spec = pl.BlockSpec((tq, grp * hd), lambda b, h, i: (b * qt + i, h))
    outs = _pcall(
        body, grid=(n_batch, n_kv, qt),
        in_specs=[q_spec, lat_spec, lat_spec, ctx_spec, ctx_spec] + cast_in,
        out_specs=[q_spec] + cast_out,
        out_shape=[jax.ShapeDtypeStruct((n_batch * seq, n_kv * grp * hd), BF16)] + cast_shapes,
        name="attn_latent")(q, k, v, k, v, *[w for w, _ in cast_srcs])
    return outs[0], outs[1:]


def _attn_context(q, k, v, *, n_batch, seq, ctx_len, n_kv, grp, hd):
    ctx_blk0 = n_batch * seq // ctx_len

    def body(q_ref, kc_ref, vc_ref, o_ref):
        keys, values = (kc_ref[...],), (vc_ref[...],)
        for g in range(grp):
            o = _softmax_pv(q_ref[:, g * hd:(g + 1) * hd], keys, values)
            o_ref[:, g * hd:(g + 1) * hd] = o.astype(o_ref.dtype)

    ctx_spec = pl.BlockSpec((ctx_len, hd), lambda b, h: (ctx_blk0 + b, h))
    return _pcall(
        body, grid=(n_batch, n_kv),
        in_specs=[pl.BlockSpec((ctx_len, grp * hd), lambda b, h: (ctx_blk0 + b, h)),
                  ctx_spec, ctx_spec],
        out_specs=pl.BlockSpec((ctx_len, grp * hd), lambda b, h: (b, h)),
        out_shape=jax.ShapeDtypeStruct((n_batch * ctx_len, n_kv * grp * hd), BF16),
        name="attn_context")(q, k, v)


def _conv_ln_silu(u, w_dw, b_dw, ln_g, ln_b, *, m, seq, ctx_len, n_lat_rows):
    cw = u.shape[1]
    taps = w_dw.shape[0]
    half = taps // 2
    ts = ctx_len
    halo_per_tile = ts // CONV_HALO
    n_halo_blocks = m // CONV_HALO
    lat_tiles = n_lat_rows // ts
    tiles_per_seq = seq // ts
    rc = 16
    conv_rows = _pick(ts, 128, 64, 32)
    conv_cols = _pick(cw, 128)
    assert CONV_HALO % 8 == 0 and half <= CONV_HALO

    def body(u_ref, prev_ref, next_ref, w_ref, b_ref, g_ref, beta_ref, o_ref, win_ref, y_ref):
        t = pl.program_id(0)
        in_lat = t < lat_tiles
        pos = t % tiles_per_seq
        is_first = jnp.logical_or(jnp.logical_not(in_lat), pos == 0)
        is_last = jnp.logical_or(jnp.logical_not(in_lat), pos == tiles_per_seq - 1)
        win_ref[0:CONV_HALO, :] = jnp.where(is_first, 0.0, prev_ref[...])
        win_ref[CONV_HALO:CONV_HALO + ts, :] = u_ref[...]
        win_ref[CONV_HALO + ts:, :] = jnp.where(is_last, 0.0, next_ref[...])
        for c0 in range(0, ts, conv_rows):
            for j0 in range(0, cw, conv_cols):
                acc = None
                for rho in range(8):
                    part = None
                    n_rows = conv_rows if rho == 0 else conv_rows + 8
                    for kk in range(taps):
                        d = CONV_HALO - half + kk
                        if d % 8 != rho:
                            continue
                        r = c0 + d - rho
                        term = win_ref[r:r + n_rows, j0:j0 + conv_cols] * w_ref[kk:kk + 1, j0:j0 + conv_cols]
                        part = term if part is None else part + term
                    if part is None:
                        continue
                    part = part[rho:rho + conv_rows, :]
                    acc = part if acc is None else acc + part
                y_ref[c0:c0 + conv_rows, j0:j0 + conv_cols] = acc + b_ref[:, j0:j0 + conv_cols]
        for c0 in range(0, ts, rc):
            y = y_ref[c0:c0 + rc, :]
            mu = jnp.mean(y, axis=-1, keepdims=True)
            yc = y - mu
            var = jnp.mean(yc * yc, axis=-1, keepdims=True)
            z = yc * lax.rsqrt(var + EPS) * g_ref[...] + beta_ref[...]
            o_ref[c0:c0 + rc, :] = (z * jax.nn.sigmoid(z)).astype(o_ref.dtype)

    vec = pl.BlockSpec((1, cw), lambda t: (0, 0))
    return _pcall(
        body, grid=(m // ts,),
        in_specs=[pl.BlockSpec((ts, cw), lambda t: (t, 0)),
                  pl.BlockSpec((CONV_HALO, cw),
                               lambda t: (jnp.maximum(t * halo_per_tile - 1, 0), 0)),
                  pl.BlockSpec((CONV_HALO, cw),
                               lambda t: (jnp.minimum((t + 1) * halo_per_tile, n_halo_blocks - 1), 0)),
                  pl.BlockSpec((taps, cw), lambda t: (0, 0)), vec, vec, vec],
        out_specs=pl.BlockSpec((ts, cw), lambda t: (t, 0)),
        out_shape=jax.ShapeDtypeStruct((m, cw), BF16),
        scratch_shapes=[pltpu.VMEM((ts + 2 * CONV_HALO, cw), F32), pltpu.VMEM((ts, cw), F32)],
        name="conv_ln_silu")(u, u, u, w_dw, b_dw, ln_g, ln_b)


def _dft_tables(n):
    j = np.arange(n, dtype=np.int64)[:, None]
    if n % 64 or n <= 64:
        ang = 2.0 * np.pi * ((j * j.T) % n) / n
        return jnp.asarray(np.cos(ang), BF16), jnp.asarray(np.sin(ang), BF16)
    lo = 64
    hi = n // lo
    a_hi = 2.0 * np.pi * ((j * lo * np.arange(hi, dtype=np.int64)[None, :]) % n) / n
    a_lo = 2.0 * np.pi * ((j * np.arange(lo, dtype=np.int64)[None, :]) % n) / n
    ch, sh = jnp.asarray(np.cos(a_hi), F32)[:, :, None], jnp.asarray(np.sin(a_hi), F32)[:, :, None]
    cl, sl = jnp.asarray(np.cos(a_lo), F32)[:, None, :], jnp.asarray(np.sin(a_lo), F32)[:, None, :]
    c = (ch * cl - sh * sl).reshape(n, n).astype(BF16)
    s = (sh * cl + ch * sl).reshape(n, n).astype(BF16)
    return c, s


def _channel_dft_matrix(gw):
    k = np.arange(gw, dtype=np.int64)
    ang = 2.0 * np.pi * ((k[:, None] * k[None, :]) % gw) / gw
    return jnp.asarray(np.concatenate([np.cos(ang), -np.sin(ang)], axis=1), BF16)


def _four_proj_dft(h, w, layer, col_off, chan_dft, *, m, width, tm):
    gw = chan_dft.shape[0]
    tn = _pick(width, 2 * gw, gw)
    sub = tm // 2

    def body(x_ref, w_ref, d_ref, re_ref, im_ref):
        for r0 in range(0, tm, sub):
            rows = slice(r0, r0 + sub)
            f = jnp.dot(x_ref[rows, :], w_ref[...], preferred_element_type=F32).astype(BF16)
            for g0 in range(0, tn, gw):
                zc = jnp.dot(f[:, g0:g0 + gw], d_ref[...], preferred_element_type=F32)
                re_ref[rows, g0:g0 + gw] = zc[:, :gw].astype(BF16)
                im_ref[rows, g0:g0 + gw] = zc[:, gw:].astype(BF16)

    out_spec = pl.BlockSpec((tm, tn), lambda j, i: (i, j))
    return _pcall(
        body, grid=(width // tn, m // tm),
        in_specs=[pl.BlockSpec((tm, h.shape[1]), lambda j, i: (i, 0)),
                  pl.BlockSpec((None, w.shape[1], tn), lambda j, i: (layer, 0, j + col_off // tn)),
                  pl.BlockSpec(chan_dft.shape, lambda j, i: (0, 0))],
        out_specs=[out_spec, out_spec],
        out_shape=[jax.ShapeDtypeStruct((m, width), BF16)] * 2,
        name="four_proj_dft")(h, w, chan_dft)


def _position_dft(re, im, cn, sn, *, n, row_blk0, n_batch, width, scale):
    tm = _pick(n, 512, 256, 128)
    tn = _pick(width, 512, 256, 128)
    mt = n // tm

    def body(c_ref, s_ref, re_ref, im_ref, o_ref):
        acc = jnp.dot(c_ref[...], re_ref[...], preferred_element_type=F32)
        acc = acc + jnp.dot(s_ref[...], im_ref[...], preferred_element_type=F32)
        o_ref[...] = (acc * scale).astype(o_ref.dtype)

    mat_spec = pl.BlockSpec((tm, n), lambda b, j, i: (i, 0))
    return _pcall(
        body, grid=(n_batch, width // tn, mt),
        in_specs=[mat_spec, mat_spec,
                  pl.BlockSpec((n, tn), lambda b, j, i: (row_blk0 + b, j)),
                  pl.BlockSpec((n, tn), lambda b, j, i: (row_blk0 + b, j))],
        out_specs=pl.BlockSpec((tm, tn), lambda b, j, i: (b * mt + i, j)),
        out_shape=jax.ShapeDtypeStruct((n_batch * n, width), BF16),
        name="position_dft")(cn, sn, re, im)


def _route(logits, n_groups, per_group):
    glog = logits[:, :n_groups]
    g_hot = jax.nn.one_hot(jnp.argmax(glog, axis=-1), n_groups, dtype=F32)
    p_grp = jnp.sum(jax.nn.softmax(glog, axis=-1) * g_hot, axis=-1, keepdims=True)
    elog = logits[:, n_groups:n_groups + n_groups * per_group].reshape(-1, n_groups, per_group)
    elog_sel = jnp.sum(elog * g_hot[:, :, None], axis=1)
    i0 = jnp.argmax(elog_sel, axis=-1)
    hot0 = jax.nn.one_hot(i0, per_group, dtype=jnp.bool_)
    rest = jnp.where(hot0, -jnp.inf, elog_sel)
    i1 = jnp.argmax(rest, axis=-1)
    top_v = jnp.stack([jnp.max(elog_sel, axis=-1), jnp.max(rest, axis=-1)], axis=-1)
    w_top = jax.nn.softmax(top_v, axis=-1) * p_grp
    gsel = jnp.argmax(glog, axis=-1)
    eid = gsel[:, None] * per_group + jnp.stack([i0, i1], axis=-1)
    return eid.astype(jnp.int32), w_top


def _dispatch_plan(eid, n_experts, tm):
    n_slots = eid.shape[0] * TOP_K
    n_tiles = n_slots // tm + n_experts
    hot = (eid.reshape(-1)[:, None] == jnp.arange(n_experts, dtype=jnp.int32)[None, :])
    hot = hot.astype(jnp.int32)
    running = jnp.cumsum(hot, axis=0)
    counts = running[-1]
    padded = ((counts + tm - 1) // tm) * tm
    pad_end = jnp.cumsum(padded)
    pad_start = pad_end - padded
    slot_pos = jnp.sum(hot * (running - 1 + pad_start[None, :]), axis=1).astype(jnp.int32)
    tile_row0 = jnp.arange(n_tiles, dtype=jnp.int32) * tm
    tile_e = jnp.sum((tile_row0[:, None] >= pad_end[None, :]).astype(jnp.int32), axis=1)
    tile_e = jnp.minimum(tile_e, n_experts - 1).astype(jnp.int32)
    tile_valid = (tile_row0 < pad_end[-1]).astype(jnp.int32)
    return slot_pos, tile_e, tile_valid, n_tiles


def _dispatch(xp, slot_pos, *, n_rows_out):
    m, dh = xp.shape
    tr = _pick(m, 256, 128)

    def body(pos_ref, x_ref, init_hbm, o_hbm, sem):
        del init_hbm
        i = pl.program_id(0)

        def row_copy(r, k):
            dst = pos_ref[(i * tr + r) * TOP_K + k]
            return pltpu.make_async_copy(x_ref.at[pl.ds(r, 1), :], o_hbm.at[pl.ds(dst, 1), :],
                                         sem.at[0])

        def issue(r, carry):
            for k in range(TOP_K):
                row_copy(r, k).start()
            return carry

        lax.fori_loop(0, tr, issue, 0, unroll=8)
        for k in range(TOP_K):
            pltpu.make_async_copy(x_ref, o_hbm.at[pl.ds(0, tr), :], sem.at[0]).wait()

    return _pcall(
        body, grid=(m // tr,), num_scalar_prefetch=1,
        in_specs=[pl.BlockSpec((tr, dh), lambda i, pos: (i, 0)),
                  pl.BlockSpec(memory_space=pl.ANY)],
        out_specs=pl.BlockSpec(memory_space=pl.ANY),
        out_shape=jax.ShapeDtypeStruct((n_rows_out, dh), U32),
        scratch_shapes=[pltpu.SemaphoreType.DMA((1,))],
        input_output_aliases={2: 0},
        name="moe_dispatch")(slot_pos, xp, jnp.zeros((n_rows_out, dh), U32))


def _experts(xs, w1, w3, w2, tile_e, tile_valid, *, n_tiles, tm):
    dh = xs.shape[1]
    D = 2 * dh
    fq = w1.shape[2] // FF_SPLITS
    cn = _pick(D, 512, 256, 128)

    def blk(i, q):
        return jnp.where(i % 2 == 0, q, FF_SPLITS - 1 - q)

    def body(te_ref, tv_ref, x_ref, w1_ref, w3_ref, w2_ref, o_ref, xlo_ref, xhi_ref, acc_ref):
        i, q = pl.program_id(0), pl.program_id(1)
        valid = tv_ref[i] == 1

        @pl.when(jnp.logical_and(valid, q == 0))
        def _():
            lo, hi = _unpack_halves(x_ref[...])
            xlo_ref[...] = lo.astype(BF16)
            xhi_ref[...] = hi.astype(BF16)

        @pl.when(valid)
        def _():
            xlo, xhi = xlo_ref[...], xhi_ref[...]
            a = (jnp.dot(xlo, w1_ref[:dh, :], preferred_element_type=F32)
                 + jnp.dot(xhi, w1_ref[dh:, :], preferred_element_type=F32))
            b = (jnp.dot(xlo, w3_ref[:dh, :], preferred_element_type=F32)
                 + jnp.dot(xhi, w3_ref[dh:, :], preferred_element_type=F32))
            hid = (a * jax.nn.sigmoid(a) * b).astype(BF16)

            @pl.when(q == 0)
            def _():
                for c in range(0, D, cn):
                    acc_ref[:, c:c + cn] = jnp.dot(hid, w2_ref[:, c:c + cn],
                                                   preferred_element_type=F32)

            @pl.when(q > 0)
            def _():
                for c in range(0, D, cn):
                    acc_ref[:, c:c + cn] += jnp.dot(hid, w2_ref[:, c:c + cn],
                                                    preferred_element_type=F32)

        @pl.when(q == FF_SPLITS - 1)
        def _():
            @pl.when(valid)
            def _():
                o_ref[...] = _pack_halves(acc_ref[...])

            @pl.when(jnp.logical_not(valid))
            def _():
                o_ref[...] = jnp.zeros(o_ref.shape, o_ref.dtype)

    return _pcall(
        body, grid=(n_tiles, FF_SPLITS), num_scalar_prefetch=2,
        in_specs=[pl.BlockSpec((tm, dh), lambda i, q, te, tv: (i, 0)),
                  pl.BlockSpec((None, D, fq), lambda i, q, te, tv: (te[i], 0, blk(i, q))),
                  pl.BlockSpec((None, D, fq), lambda i, q, te, tv: (te[i], 0, blk(i, q))),
                  pl.BlockSpec((None, fq, D), lambda i, q, te, tv: (te[i], blk(i, q), 0))],
        out_specs=pl.BlockSpec((tm, dh), lambda i, q, te, tv: (i, 0)),
        out_shape=jax.ShapeDtypeStruct((n_tiles * tm, dh), U32),
        scratch_shapes=[pltpu.VMEM((tm, dh), BF16), pltpu.VMEM((tm, dh), BF16),
                        pltpu.VMEM((tm, D), F32)],
        name="experts")(tile_e, tile_valid, xs, w1, w3, w2)


def _combine(y_sorted, slot_pos, w_top, z, mods, gate_idx, final_g, *, m, seq, n_batch):
    D = z.shape[1]
    dh = D // 2
    tc = _pick(m, 256, 128)
    n_steps = m // tc

    def body(pos_ref, y_hbm, w_ref, z_ref, g_ref, fg_ref, o_ref, buf_ref, sem):
        i = pl.program_id(0)

        def row_copy(step, slot, r, k):
            src = pos_ref[(step * tc + r) * TOP_K + k]
            return pltpu.make_async_copy(y_hbm.at[pl.ds(src, 1), :],
                                         buf_ref.at[slot, k, pl.ds(r, 1), :], sem.at[slot])

        def start_gather(step, slot):
            def issue(r, carry):
                for k in range(TOP_K):
                    row_copy(step, slot, r, k).start()
                return carry
            lax.fori_loop(0, tc, issue, 0, unroll=8)

        def wait_gather(slot):
            for k in range(TOP_K):
                pltpu.make_async_copy(y_hbm.at[pl.ds(0, tc), :], buf_ref.at[slot, k],
                                      sem.at[slot]).wait()

        slot = i % 2

        @pl.when(i == 0)
        def _():
            start_gather(0, 0)

        @pl.when(i + 1 < n_steps)
        def _():
            start_gather(i + 1, 1 - slot)

        wait_gather(slot)
        wv = w_ref[...]
        y_lo = jnp.zeros((tc, dh), F32)
        y_hi = jnp.zeros((tc, dh), F32)
        for k in range(TOP_K):
            lo, hi = _unpack_halves(buf_ref[slot, k])
            y_lo = y_lo + wv[:, k:k + 1] * lo
            y_hi = y_hi + wv[:, k:k + 1] * hi
        z_lo = z_ref[:, :dh] + g_ref[:, :dh] * y_lo
        z_hi = z_ref[:, dh:] + g_ref[:, dh:] * y_hi
        if final_g is not None:
            ssq = (jnp.sum(z_lo * z_lo, axis=-1, keepdims=True)
                   + jnp.sum(z_hi * z_hi, axis=-1, keepdims=True))
            inv = lax.rsqrt(ssq / D + EPS)
            z_lo = z_lo * inv * fg_ref[:, :dh]
            z_hi = z_hi * inv * fg_ref[:, dh:]
        o_ref[:, :dh] = z_lo
        o_ref[:, dh:] = z_hi

    fg = jnp.ones((1, D), F32) if final_g is None else final_g
    return _pcall(
        body, grid=(n_steps,), num_scalar_prefetch=1,
        in_specs=[pl.BlockSpec(memory_space=pl.ANY),
                  pl.BlockSpec((tc, TOP_K), lambda i, pos: (i, 0)),
                  pl.BlockSpec((tc, D), lambda i, pos: (i, 0)),
                  _mod_spec(gate_idx, tc, D, seq, n_batch),
                  pl.BlockSpec((1, D), lambda i, pos: (0, 0))],
        out_specs=pl.BlockSpec((tc, D), lambda i, pos: (i, 0)),
        out_shape=jax.ShapeDtypeStruct((m, D), F32),
        scratch_shapes=[pltpu.VMEM((2, TOP_K, tc, dh), U32), pltpu.SemaphoreType.DMA((2,))],
        name="moe_combine")(slot_pos, y_sorted, w_top, z, mods, fg)


def _rope_tables(seq, head_dim, n_batch, n_ctx_rows):
    half = head_dim // 2
    inv = ROPE_THETA ** (-jnp.arange(0, half, 2, dtype=F32) / half)
    t = jnp.arange(seq)
    row_ang = (t // GRID_W).astype(F32)[:, None] * inv
    col_ang = (t % GRID_W).astype(F32)[:, None] * inv
    cr, sr, cc, sc = jnp.cos(row_ang), jnp.sin(row_ang), jnp.cos(col_ang), jnp.sin(col_ang)
    cos = jnp.concatenate([cr, cr, cc, cc], axis=-1)
    sin = jnp.concatenate([-sr, sr, -sc, sc], axis=-1)
    cos = jnp.concatenate([jnp.tile(cos, (n_batch, 1)), jnp.ones((n_ctx_rows, head_dim), F32)])
    sin = jnp.concatenate([jnp.tile(sin, (n_batch, 1)), jnp.zeros((n_ctx_rows, head_dim), F32)])
    return cos, sin


def kernel(x, c, ctx, c_ctx, w_mod, b_mod, norm1_g, norm2_g, w_in, b_gate, q_gain, k_gain,
           w_dw, b_dw, conv_ln_g, conv_ln_b, w_attn_o, w_conv_o, w_four_o, w_out,
           w_grp, b_grp, w_rt, b_rt, w1, w3, w2, final_g):
    n_batch, seq, D = x.shape
    ctx_len = ctx.shape[1]
    depth = w_mod.shape[0]
    hd = q_gain.shape[-1]
    q_w = w_attn_o.shape[1]
    conv_w = w_conv_o.shape[1]
    four_w = w_four_o.shape[1]
    kv_w = (w_in.shape[2] - q_w - 2 * conv_w - four_w - 3 * D) // 2
    n_kv = kv_w // hd
    grp = q_w // kv_w
    n_groups = w_grp.shape[2]
    n_experts = w_rt.shape[2]
    per_group = n_experts // n_groups
    o_k, o_v, o_a = q_w, q_w + kv_w, q_w + 2 * kv_w
    o_b, o_f, o_g = o_a + conv_w, o_a + 2 * conv_w, o_a + 2 * conv_w + four_w

    m_lat = n_batch * seq
    m_ctx = n_batch * ctx_len
    m_all = m_lat + m_ctx
    tm = _pick(m_ctx, 512, 256, 128)
    assert seq % tm == 0 and m_ctx % tm == 0 and seq % ctx_len == 0 and seq % GRID_W == 0
    n_vec = n_batch + 1
    assert n_vec <= 8

    z, z_tail = x.reshape(m_lat, D), ctx.reshape(m_ctx, D)
    cs_t = jnp.zeros((D, 8), F32).at[:, :n_batch].set(c.T).at[:, n_batch].set(c_ctx)
    cos_t, sin_t = _rope_tables(seq, hd, n_batch, m_ctx)
    cn_lat, sn_lat = _dft_tables(seq)
    cn_ctx, sn_ctx = _dft_tables(ctx_len)
    gw = four_w // FOUR_GROUPS
    chan_dft = _channel_dft_matrix(gw)
    seg = dict(seq=seq, n_batch=n_batch)
    plain = lambda dots: dots[0]

    w_in_b, w_out_b = w_in.astype(BF16), w_out.astype(BF16)
    w_attn_o_b, w_conv_o_b, w_four_o_b = (w_attn_o.astype(BF16), w_conv_o.astype(BF16),
                                          w_four_o.astype(BF16))
    ff = w1.shape[3]
    expert_srcs = [(w1.reshape(depth * n_experts * D, ff), n_experts * D),
                   (w3.reshape(depth * n_experts * D, ff), n_experts * D),
                   (w2.reshape(depth * n_experts * ff, D), n_experts * ff)]
    b_mod3 = b_mod[:, None, :]

    for l in range(depth):
        last = l == depth - 1
        m = m_lat if last else m_all
        mods = _ada(cs_t, w_mod, b_mod3, l, n_vec)[:n_vec].reshape(n_vec * N_MOD, 1, D)

        h = _norm_mod(z, z_tail, norm1_g[l][None, :], mods, 0, 1, m=m_all, **seg)

        tn_h = _pick(math.gcd(q_w, kv_w), 512, 256, 128)
        rope = [("rows", cos_t), ("rows", sin_t)]
        q = _matmul([h], [(w_in_b, l, 0, 0)], [("full", q_gain[l][None, :])] + rope,
                    _head_norm_rope_epilogue(hd, hd ** -0.5 * math.log2(math.e)),
                    m=m, n=q_w, tm=tm, tn=tn_h, out_dtype=BF16, name="q_proj", row_split=4)
        k = _matmul([h], [(w_in_b, l, 0, o_k // tn_h)], [("full", k_gain[l][None, :])] + rope,
                    _head_norm_rope_epilogue(hd, 1.0),
                    m=m_all, n=kv_w, tm=tm, tn=tn_h, out_dtype=BF16, name="k_proj", row_split=2)
        v = _matmul([h], [(w_in_b, l, 0, o_v // tn_h)], [], plain,
                    m=m_all, n=kv_w, tm=tm, tn=tn_h, out_dtype=BF16, name="v_proj")
        geo = dict(n_batch=n_batch, seq=seq, ctx_len=ctx_len, n_kv=n_kv, grp=grp, hd=hd)
        att, (w1_b, w3_b, w2_b) = _attn_latent(q, k, v, expert_srcs, l, **geo)
        w1_b = w1_b.reshape(n_experts, D, ff)
        w3_b = w3_b.reshape(n_experts, D, ff)
        w2_b = w2_b.reshape(n_experts, ff, D)
        if not last:
            att = jnp.concatenate([att, _attn_context(q, k, v, **geo)], axis=0)

        tn_c = _pick(math.gcd(conv_w, four_w, o_a), 512, 256, 128)
        u = _matmul([h], [(w_in_b, l, 0, o_a // tn_c), (w_in_b, l, 0, o_b // tn_c)], [],
                    lambda dots: dots[0] * jax.nn.sigmoid(dots[1]),
                    m=m, n=conv_w, tm=tm, tn=tn_c, out_dtype=F32, name="glu_proj")
        conv = _conv_ln_silu(u, w_dw[l], b_dw[l][None, :], conv_ln_g[l][None, :],
                             conv_ln_b[l][None, :], m=m, seq=seq, ctx_len=ctx_len,
                             n_lat_rows=m_lat)

        f_re, f_im = _four_proj_dft(h, w_in_b, l, o_f, chan_dft, m=m, width=four_w, tm=tm)
        four = _position_dft(f_re, f_im, cn_lat, sn_lat, n=seq, row_blk0=0, n_batch=n_batch,
                             width=four_w, scale=float((seq * gw) ** -0.5))
        if not last:
            four_ctx = _position_dft(f_re, f_im, cn_ctx, sn_ctx, n=ctx_len, row_blk0=m_lat // ctx_len,
                                     n_batch=n_batch, width=four_w,
                                     scale=float((ctx_len * gw) ** -0.5))
            four = jnp.concatenate([four, four_ctx], axis=0)

        tn_m = _pick(D, 256, 128)
        d_t, g_t = D // tn_m, o_g // tn_m

        def merge(dots, b0, b1, b2):
            return (jax.nn.sigmoid(dots[0] + b0) * dots[1] + jax.nn.sigmoid(dots[2] + b1) * dots[3]
                    + jax.nn.sigmoid(dots[4] + b2) * dots[5])

        bg = b_gate[l][None, :]
        merged = _matmul(
            [h, att, conv, four],
            [(w_in_b, l, 0, g_t), (w_attn_o_b, l, 1, 0),
             (w_in_b, l, 0, g_t + d_t), (w_conv_o_b, l, 2, 0),
             (w_in_b, l, 0, g_t + 2 * d_t), (w_four_o_b, l, 3, 0)],
            [("col", bg, 0), ("col", bg, d_t), ("col", bg, 2 * d_t)], merge,
            m=m, n=D, tm=tm, tn=tn_m, out_dtype=BF16, name="gated_merge")

        tn_o = _pick(D, 1024, 512, 256, 128)
        resid = ("tile2", z, z_tail) if z_tail is not None and m > z.shape[0] else ("tile", z)
        z1 = _matmul([merged], [(w_out_b, l, 0, 0)], [resid, ("mod", mods, 2)],
                     lambda dots, zt, g: zt + g * dots[0],
                     m=m, n=D, tm=tm, tn=tn_o, out_dtype=F32, name="out_proj", **seg)

        w_router = jnp.zeros((D, ROUTER_LANES), F32)
        w_router = w_router.at[:, :n_groups].set(w_grp[l]).at[:, n_groups:n_groups + n_experts].set(w_rt[l])
        b_router = jnp.zeros((1, ROUTER_LANES), F32)
        b_router = b_router.at[0, :n_groups].set(b_grp[l]).at[0, n_groups:n_groups + n_experts].set(b_rt[l])
        xp, logits = _norm_mod_router(z1, norm2_g[l][None, :], mods, 3, 4, w_router, b_router,
                                      m=m, **seg)
        eid, w_top = _route(logits, n_groups, per_group)
        tm_e = _pick(m * TOP_K, 512, 256, 128)
        slot_pos, tile_e, tile_valid, n_tiles = _dispatch_plan(eid, n_experts, tm_e)
        xs = _dispatch(xp, slot_pos, n_rows_out=n_tiles * tm_e)
        y_sorted = _experts(xs, w1_b, w3_b, w2_b, tile_e, tile_valid, n_tiles=n_tiles, tm=tm_e)
        z = _combine(y_sorted, slot_pos, w_top, z1, mods, 5,
                     final_g[None, :] if last else None, m=m, **seg)
        z_tail = None

    return z.reshape(n_batch, seq, D)
```

```python
import math

import numpy as np
import jax
import jax.numpy as jnp
from jax import lax
from jax.experimental import pallas as pl
from jax.experimental.pallas import tpu as pltpu

F32 = jnp.float32
BF16 = jnp.bfloat16
U32 = jnp.uint32

EPS = 1e-6
GRID_W = 64
ROPE_THETA = 10000.0
TOP_K = 2
FOUR_GROUPS = 4
N_MOD = 6
CONV_HALO = 16
ROUTER_LANES = 128
FF_SPLITS = 4

V7X_VMEM_LIMIT_BYTES = 56 * 1024 * 1024


def _pcall(body, *, grid, in_specs, out_specs, out_shape, scratch_shapes=(), name=None,
           num_scalar_prefetch=0, input_output_aliases=None):
    params = pltpu.CompilerParams(dimension_semantics=("arbitrary",) * len(grid),
                                  vmem_limit_bytes=V7X_VMEM_LIMIT_BYTES)
    aliases = input_output_aliases or {}
    if num_scalar_prefetch:
        grid_spec = pltpu.PrefetchScalarGridSpec(
            num_scalar_prefetch=num_scalar_prefetch, grid=grid, in_specs=in_specs,
            out_specs=out_specs, scratch_shapes=scratch_shapes)
        return pl.pallas_call(body, grid_spec=grid_spec, out_shape=out_shape,
                              compiler_params=params, name=name, input_output_aliases=aliases)
    return pl.pallas_call(body, grid=grid, in_specs=in_specs, out_specs=out_specs,
                          out_shape=out_shape, scratch_shapes=scratch_shapes,
                          compiler_params=params, name=name, input_output_aliases=aliases)


def _pick(n, *cands):
    for c in cands:
        if n % c == 0:
            return c
    return n


def _pack_halves(x):
    half = x.shape[1] // 2
    lo = lax.bitcast_convert_type(x[:, :half].astype(BF16).astype(F32), U32)
    hi = lax.bitcast_convert_type(x[:, half:].astype(BF16).astype(F32), U32)
    return hi | (lo >> 16)


def _unpack_halves(p):
    lo = lax.bitcast_convert_type(p << 16, F32)
    hi = lax.bitcast_convert_type(p & jnp.uint32(0xFFFF0000), F32)
    return lo, hi


def _ada(cs_t, w, b, layer, n_vec):
    _, D, N = w.shape
    tk = _pick(D, 512, 256, 128)
    tn = _pick(N, 2048, 1024, 512, 256, 128)

    def body(c_ref, w_ref, b_ref, o_ref):
        @pl.when(pl.program_id(1) == 0)
        def _():
            o_ref[...] = jnp.broadcast_to(b_ref[...], o_ref.shape)

        c = c_ref[...]
        s = c * jax.nn.sigmoid(c)
        wv = w_ref[...]
        rows = [jnp.sum(s[:, r:r + 1] * wv, axis=0, keepdims=True) for r in range(n_vec)]
        rows.append(jnp.zeros((8 - n_vec, tn), F32))
        o_ref[...] += jnp.concatenate(rows, axis=0)

    return _pcall(
        body, grid=(N // tn, D // tk),
        in_specs=[pl.BlockSpec((tk, 8), lambda j, k: (k, 0)),
                  pl.BlockSpec((None, tk, tn), lambda j, k: (layer, k, j)),
                  pl.BlockSpec((None, 1, tn), lambda j, k: (layer, 0, j))],
        out_specs=pl.BlockSpec((8, tn), lambda j, k: (0, j)),
        out_shape=jax.ShapeDtypeStruct((8, N), F32), name="ada")(cs_t, w, b)


def _seg_of_row(row0, seq, n_batch):
    return jnp.minimum(row0 // seq, n_batch)


def _mod_spec(idx, tr, width, seq, n_batch):
    return pl.BlockSpec((None, 1, width),
                        lambda i, *_: (_seg_of_row(i * tr, seq, n_batch) * N_MOD + idx, 0, 0))


def _norm_mod(x, x_tail, g, mods, sh_idx, sc_idx, *, m, seq, n_batch):
    D = x.shape[1]
    tr = _pick(m, 256, 128)
    head_tiles = x.shape[0] // tr
    if x_tail is None:
        x_tail = x

    def body(x_ref, t_ref, g_ref, sh_ref, sc_ref, o_ref):
        xf = jnp.where(pl.program_id(0) < head_tiles, x_ref[...], t_ref[...])
        y = xf * lax.rsqrt(jnp.mean(xf * xf, axis=-1, keepdims=True) + EPS) * g_ref[...]
        o_ref[...] = (y * (1.0 + sc_ref[...]) + sh_ref[...]).astype(o_ref.dtype)

    return _pcall(
        body, grid=(m // tr,),
        in_specs=[pl.BlockSpec((tr, D), lambda i: (jnp.minimum(i, head_tiles - 1), 0)),
                  pl.BlockSpec((tr, D), lambda i: (jnp.maximum(i - head_tiles, 0), 0)),
                  pl.BlockSpec((1, D), lambda i: (0, 0)),
                  _mod_spec(sh_idx, tr, D, seq, n_batch),
                  _mod_spec(sc_idx, tr, D, seq, n_batch)],
        out_specs=pl.BlockSpec((tr, D), lambda i: (i, 0)),
        out_shape=jax.ShapeDtypeStruct((m, D), BF16), name="norm_mod")(x, x_tail, g, mods, mods)


def _norm_mod_router(x, g, mods, sh_idx, sc_idx, w_router, b_router, *, m, seq, n_batch):
    D = x.shape[1]
    tr = _pick(m, 256, 128)

    w_hi = w_router.astype(BF16)
    w_lo = (w_router - w_hi.astype(F32)).astype(BF16)

    def body(x_ref, g_ref, sh_ref, sc_ref, whi_ref, wlo_ref, br_ref, o_ref, l_ref):
        xf = x_ref[...]
        y = xf * lax.rsqrt(jnp.mean(xf * xf, axis=-1, keepdims=True) + EPS) * g_ref[...]
        h = y * (1.0 + sc_ref[...]) + sh_ref[...]
        o_ref[...] = _pack_halves(h)
        h_hi = h.astype(BF16)
        h_lo = (h - h_hi.astype(F32)).astype(BF16)
        small = (jnp.dot(h_lo, whi_ref[...], preferred_element_type=F32)
                 + jnp.dot(h_hi, wlo_ref[...], preferred_element_type=F32))
        l_ref[...] = jnp.dot(h_hi, whi_ref[...], preferred_element_type=F32) + small + br_ref[...]

    return _pcall(
        body, grid=(m // tr,),
        in_specs=[pl.BlockSpec((tr, D), lambda i: (i, 0)),
                  pl.BlockSpec((1, D), lambda i: (0, 0)),
                  _mod_spec(sh_idx, tr, D, seq, n_batch),
                  _mod_spec(sc_idx, tr, D, seq, n_batch),
                  pl.BlockSpec((D, ROUTER_LANES), lambda i: (0, 0)),
                  pl.BlockSpec((D, ROUTER_LANES), lambda i: (0, 0)),
                  pl.BlockSpec((1, ROUTER_LANES), lambda i: (0, 0))],
        out_specs=[pl.BlockSpec((tr, D // 2), lambda i: (i, 0)),
                   pl.BlockSpec((tr, ROUTER_LANES), lambda i: (i, 0))],
        out_shape=[jax.ShapeDtypeStruct((m, D // 2), U32),
                   jax.ShapeDtypeStruct((m, ROUTER_LANES), F32)],
        name="norm_mod_router")(x, g, mods, mods, w_hi, w_lo, b_router)


def _matmul(xs, ws, extras, epilogue, *, m, n, tm, tn, out_dtype, name, seq=1, n_batch=0,
            row_split=1):
    in_specs, args = [], []
    for x in xs:
        in_specs.append(pl.BlockSpec((tm, x.shape[1]), lambda j, i: (i, 0)))
        args.append(x)
    for w, layer, _, off in ws:
        in_specs.append(pl.BlockSpec((None, w.shape[1], tn),
                                     lambda j, i, layer=layer, off=off: (layer, 0, j + off)))
        args.append(w)
    for ex in extras:
        kind, arr = ex[0], ex[1]
        if kind == "col":
            in_specs.append(pl.BlockSpec((1, tn), lambda j, i, off=ex[2]: (0, j + off)))
        elif kind == "tile":
            in_specs.append(pl.BlockSpec((tm, tn), lambda j, i: (i, j)))
        elif kind == "tile2":
            head_tiles = arr.shape[0] // tm
            in_specs.append(pl.BlockSpec(
                (tm, tn), lambda j, i, ht=head_tiles: (jnp.minimum(i, ht - 1), j)))
            args.append(arr)
            in_specs.append(pl.BlockSpec(
                (tm, tn), lambda j, i, ht=head_tiles: (jnp.maximum(i - ht, 0), j)))
            arr = ex[2]
        elif kind == "rows":
            in_specs.append(pl.BlockSpec((tm, arr.shape[1]), lambda j, i: (i, 0)))
        elif kind == "full":
            in_specs.append(pl.BlockSpec(arr.shape, lambda j, i: (0, 0)))
        elif kind == "mod":
            in_specs.append(pl.BlockSpec(
                (None, 1, tn),
                lambda j, i, idx=ex[2]: (_seg_of_row(i * tm, seq, n_batch) * N_MOD + idx, 0, j)))
        args.append(arr)
    nx, nw = len(xs), len(ws)

    sub = tm // row_split

    def body(*refs):
        x_refs, w_refs = refs[:nx], refs[nx:nx + nw]
        e_refs, o_ref = list(refs[nx + nw:-1]), refs[-1]
        i = pl.program_id(1)
        for r0 in range(0, tm, sub):
            rows = slice(r0, r0 + sub)
            dots = [jnp.dot(x_refs[xi][rows, :], w_refs[p][...], preferred_element_type=F32)
                    for p, (_, _, xi, _) in enumerate(ws)]
            vals, e = [], 0
            for ex in extras:
                if ex[0] == "tile2":
                    in_head = i < ex[1].shape[0] // tm
                    vals.append(jnp.where(in_head, e_refs[e][rows, :], e_refs[e + 1][rows, :]))
                    e += 2
                    continue
                vals.append(e_refs[e][rows, :] if ex[0] in ("tile", "rows") else e_refs[e][...])
                e += 1
            o_ref[rows, :] = epilogue(dots, *vals).astype(o_ref.dtype)

    return _pcall(body, grid=(n // tn, m // tm), in_specs=in_specs,
                  out_specs=pl.BlockSpec((tm, tn), lambda j, i: (i, j)),
                  out_shape=jax.ShapeDtypeStruct((m, n), out_dtype), name=name)(*args)


def _head_norm_rope_epilogue(head_dim, scale):
    quarter = head_dim // 4

    def epilogue(dots, gain, cos, sin_signed):
        d = dots[0]
        lane = lax.broadcasted_iota(jnp.int32, (d.shape[0], head_dim), 1)
        first_half = (lane % (2 * quarter)) < quarter
        outs = []
        for h in range(d.shape[1] // head_dim):
            xh = d[:, h * head_dim:(h + 1) * head_dim]
            y = xh * lax.rsqrt(jnp.mean(xh * xh, axis=-1, keepdims=True) + EPS) * gain
            partner = jnp.where(first_half, pltpu.roll(y, head_dim - quarter, 1),
                                pltpu.roll(y, quarter, 1))
            outs.append((y * cos + partner * sin_signed) * scale)
        return jnp.concatenate(outs, axis=-1)

    return epilogue


def _softmax_pv(q, keys, values):
    dn = (((1,), (1,)), ((), ()))
    s = [lax.dot_general(q, k, dn, preferred_element_type=F32) for k in keys]
    m = s[0].max(axis=-1, keepdims=True)
    for si in s[1:]:
        m = jnp.maximum(m, si.max(axis=-1, keepdims=True))
    p = [jnp.exp2(si - m) for si in s]
    l = p[0].sum(axis=-1, keepdims=True)
    for pi in p[1:]:
        l = l + pi.sum(axis=-1, keepdims=True)
    o = jnp.dot(p[0].astype(BF16), values[0], preferred_element_type=F32)
    for pi, v in zip(p[1:], values[1:]):
        o = o + jnp.dot(pi.astype(BF16), v, preferred_element_type=F32)
    return o / l


def _attn_latent(q, k, v, cast_srcs, *, n_batch, seq, ctx_len, n_kv, grp, hd):
    tq = _pick(seq, 256, 128)
    ck = _pick(seq, 256, 128)
    qt = seq // tq
    ctx_blk0 = n_batch * seq // ctx_len
    n_steps = n_batch * n_kv * qt
    n_cast = len(cast_srcs)
    dn = (((1,), (1,)), ((), ()))

    def body(q_ref, kl_ref, vl_ref, kc_ref, vc_ref, *rest):
        src_refs, o_ref, dst_refs = rest[:n_cast], rest[n_cast], rest[n_cast + 1:]
        for s_ref, d_ref in zip(src_refs, dst_refs):
            d_ref[...] = s_ref[...].astype(BF16)

        qs = jnp.concatenate([q_ref[:, g * hd:(g + 1) * hd] for g in range(grp)], axis=0)
        chunks = [(kc_ref, vc_ref, 0, ctx_len)]
        chunks += [(kl_ref, vl_ref, c0, ck) for c0 in range(0, seq, ck)]
        m = l = acc = None
        for kr, vr, c0, n in chunks:
            s = lax.dot_general(qs, kr[c0:c0 + n, :], dn, preferred_element_type=F32)
            mc = s.max(axis=-1, keepdims=True)
            if m is None:
                m_new = mc
                p = jnp.exp2(s - m_new)
                l = p.sum(axis=-1, keepdims=True)
                acc = jnp.dot(p.astype(BF16), vr[c0:c0 + n, :], preferred_element_type=F32)
            else:
                m_new = jnp.maximum(m, mc)
                alpha = jnp.exp2(m - m_new)
                p = jnp.exp2(s - m_new)
                l = alpha * l + p.sum(axis=-1, keepdims=True)
                acc = alpha * acc + jnp.dot(p.astype(BF16), vr[c0:c0 + n, :],
                                            preferred_element_type=F32)
            m = m_new
        o = acc / l
        for g in range(grp):
            o_ref[:, g * hd:(g + 1) * hd] = o[g * tq:(g + 1) * tq, :].astype(o_ref.dtype)

    def step(b, h, i):
        return (b * n_kv + h) * qt + i

    lat_spec = pl.BlockSpec((seq, hd), lambda b, h, i: (b, h))
    ctx_spec = pl.BlockSpec((ctx_len, hd), lambda b, h, i: (ctx_blk0 + b, h))
    cast_in, cast_out, cast_shapes = [], [], []
    for w, rows, layer in cast_srcs:
        assert rows % (16 * n_steps) == 0, (rows, n_steps)
        blk = (rows // n_steps, w.shape[1])
        cast_in.append(pl.BlockSpec(
            blk, lambda b, h, i, layer=layer: (layer * n_steps + step(b, h, i), 0)))
        cast_out.append(pl.BlockSpec(blk, lambda b, h, i: (step(b, h, i), 0)))
        cast_shapes.append(jax.ShapeDtypeStruct((rows, w.shape[1]), BF16))
    q_spec = pl.BlockSpec((tq, grp * hd), lambda b, h, i: (b * qt + i, h))
    outs = _pcall(
        body, grid=(n_batch, n_kv, qt),
        in_specs=[q_spec, lat_spec, lat_spec, ctx_spec, ctx_spec] + cast_in,
        out_specs=[q_spec] + cast_out,
        out_shape=[jax.ShapeDtypeStruct((n_batch * seq, n_kv * grp * hd), BF16)] + cast_shapes,
        name="attn_latent")(q, k, v, k, v, *[w for w, _, _ in cast_srcs])
    return outs[0], outs[1:]


def _attn_context(q, k, v, *, n_batch, seq, ctx_len, n_kv, grp, hd):
    ctx_blk0 = n_batch * seq // ctx_len

    def body(q_ref, kc_ref, vc_ref, o_ref):
        keys, values = (kc_ref[...],), (vc_ref[...],)
        for g in range(grp):
            o = _softmax_pv(q_ref[:, g * hd:(g + 1) * hd], keys, values)
            o_ref[:, g * hd:(g + 1) * hd] = o.astype(o_ref.dtype)

    ctx_spec = pl.BlockSpec((ctx_len, hd), lambda b, h: (ctx_blk0 + b, h))
    return _pcall(
        body, grid=(n_batch, n_kv),
        in_specs=[pl.BlockSpec((ctx_len, grp * hd), lambda b, h: (ctx_blk0 + b, h)),
                  ctx_spec, ctx_spec],
        out_specs=pl.BlockSpec((ctx_len, grp * hd), lambda b, h: (b, h)),
        out_shape=jax.ShapeDtypeStruct((n_batch * ctx_len, n_kv * grp * hd), BF16),
        name="attn_context")(q, k, v)


def _conv_ln_silu(u, w_dw, b_dw, ln_g, ln_b, *, m, seq, ctx_len, n_lat_rows):
    cw = u.shape[1]
    taps = w_dw.shape[0]
    half = taps // 2
    ts = ctx_len
    halo_per_tile = ts // CONV_HALO
    n_halo_blocks = m // CONV_HALO
    lat_tiles = n_lat_rows // ts
    tiles_per_seq = seq // ts
    rc = 16
    conv_rows = _pick(ts, 128, 64, 32)
    conv_cols = _pick(cw, 128)
    assert CONV_HALO % 8 == 0 and half <= CONV_HALO

    def body(u_ref, prev_ref, next_ref, w_ref, b_ref, g_ref, beta_ref, o_ref, win_ref, y_ref):
        t = pl.program_id(0)
        in_lat = t < lat_tiles
        pos = t % tiles_per_seq
        is_first = jnp.logical_or(jnp.logical_not(in_lat), pos == 0)
        is_last = jnp.logical_or(jnp.logical_not(in_lat), pos == tiles_per_seq - 1)
        win_ref[0:CONV_HALO, :] = jnp.where(is_first, 0.0, prev_ref[...])
        win_ref[CONV_HALO:CONV_HALO + ts, :] = u_ref[...]
        win_ref[CONV_HALO + ts:, :] = jnp.where(is_last, 0.0, next_ref[...])
        for c0 in range(0, ts, conv_rows):
            for j0 in range(0, cw, conv_cols):
                acc = None
                for rho in range(8):
                    part = None
                    n_rows = conv_rows if rho == 0 else conv_rows + 8
                    for kk in range(taps):
                        d = CONV_HALO - half + kk
                        if d % 8 != rho:
                            continue
                        r = c0 + d - rho
                        term = win_ref[r:r + n_rows, j0:j0 + conv_cols] * w_ref[kk:kk + 1, j0:j0 + conv_cols]
                        part = term if part is None else part + term
                    if part is None:
                        continue
                    part = part[rho:rho + conv_rows, :]
                    acc = part if acc is None else acc + part
                y_ref[c0:c0 + conv_rows, j0:j0 + conv_cols] = acc + b_ref[:, j0:j0 + conv_cols]
        for c0 in range(0, ts, rc):
            y = y_ref[c0:c0 + rc, :]
            mu = jnp.mean(y, axis=-1, keepdims=True)
            yc = y - mu
            var = jnp.mean(yc * yc, axis=-1, keepdims=True)
            z = yc * lax.rsqrt(var + EPS) * g_ref[...] + beta_ref[...]
            o_ref[c0:c0 + rc, :] = (z * jax.nn.sigmoid(z)).astype(o_ref.dtype)

    vec = pl.BlockSpec((1, cw), lambda t: (0, 0))
    return _pcall(
        body, grid=(m // ts,),
        in_specs=[pl.BlockSpec((ts, cw), lambda t: (t, 0)),
                  pl.BlockSpec((CONV_HALO, cw),
                               lambda t: (jnp.maximum(t * halo_per_tile - 1, 0), 0)),
                  pl.BlockSpec((CONV_HALO, cw),
                               lambda t: (jnp.minimum((t + 1) * halo_per_tile, n_halo_blocks - 1), 0)),
                  pl.BlockSpec((taps, cw), lambda t: (0, 0)), vec, vec, vec],
        out_specs=pl.BlockSpec((ts, cw), lambda t: (t, 0)),
        out_shape=jax.ShapeDtypeStruct((m, cw), BF16),
        scratch_shapes=[pltpu.VMEM((ts + 2 * CONV_HALO, cw), F32), pltpu.VMEM((ts, cw), F32)],
        name="conv_ln_silu")(u, u, u, w_dw, b_dw, ln_g, ln_b)


def _dft_tables(n):
    j = np.arange(n, dtype=np.int64)[:, None]
    if n % 64 or n <= 64:
        ang = 2.0 * np.pi * ((j * j.T) % n) / n
        return jnp.asarray(np.cos(ang), BF16), jnp.asarray(np.sin(ang), BF16)
    lo = 64
    hi = n // lo
    a_hi = 2.0 * np.pi * ((j * lo * np.arange(hi, dtype=np.int64)[None, :]) % n) / n
    a_lo = 2.0 * np.pi * ((j * np.arange(lo, dtype=np.int64)[None, :]) % n) / n
    ch, sh = jnp.asarray(np.cos(a_hi), F32)[:, :, None], jnp.asarray(np.sin(a_hi), F32)[:, :, None]
    cl, sl = jnp.asarray(np.cos(a_lo), F32)[:, None, :], jnp.asarray(np.sin(a_lo), F32)[:, None, :]
    c = (ch * cl - sh * sl).reshape(n, n).astype(BF16)
    s = (sh * cl + ch * sl).reshape(n, n).astype(BF16)
    return c, s


def _channel_dft_matrix(gw):
    k = np.arange(gw, dtype=np.int64)
    ang = 2.0 * np.pi * ((k[:, None] * k[None, :]) % gw) / gw
    return jnp.asarray(np.concatenate([np.cos(ang), -np.sin(ang)], axis=1), BF16)


def _four_proj_dft(h, w, layer, col_off, chan_dft, *, m, width, tm):
    gw = chan_dft.shape[0]
    tn = _pick(width, 2 * gw, gw)
    sub = tm // 2

    def body(x_ref, w_ref, d_ref, re_ref, im_ref):
        for r0 in range(0, tm, sub):
            rows = slice(r0, r0 + sub)
            f = jnp.dot(x_ref[rows, :], w_ref[...], preferred_element_type=F32).astype(BF16)
            for g0 in range(0, tn, gw):
                zc = jnp.dot(f[:, g0:g0 + gw], d_ref[...], preferred_element_type=F32)
                re_ref[rows, g0:g0 + gw] = zc[:, :gw].astype(BF16)
                im_ref[rows, g0:g0 + gw] = zc[:, gw:].astype(BF16)

    out_spec = pl.BlockSpec((tm, tn), lambda j, i: (i, j))
    return _pcall(
        body, grid=(width // tn, m // tm),
        in_specs=[pl.BlockSpec((tm, h.shape[1]), lambda j, i: (i, 0)),
                  pl.BlockSpec((None, w.shape[1], tn), lambda j, i: (layer, 0, j + col_off // tn)),
                  pl.BlockSpec(chan_dft.shape, lambda j, i: (0, 0))],
        out_specs=[out_spec, out_spec],
        out_shape=[jax.ShapeDtypeStruct((m, width), BF16)] * 2,
        name="four_proj_dft")(h, w, chan_dft)


def _position_dft(re, im, cn, sn, *, n, row_blk0, n_batch, width, scale):
    tm = _pick(n, 512, 256, 128)
    tn = _pick(width, 512, 256, 128)
    mt = n // tm

    def body(c_ref, s_ref, re_ref, im_ref, o_ref):
        acc = jnp.dot(c_ref[...], re_ref[...], preferred_element_type=F32)
        acc = acc + jnp.dot(s_ref[...], im_ref[...], preferred_element_type=F32)
        o_ref[...] = (acc * scale).astype(o_ref.dtype)

    mat_spec = pl.BlockSpec((tm, n), lambda b, j, i: (i, 0))
    return _pcall(
        body, grid=(n_batch, width // tn, mt),
        in_specs=[mat_spec, mat_spec,
                  pl.BlockSpec((n, tn), lambda b, j, i: (row_blk0 + b, j)),
                  pl.BlockSpec((n, tn), lambda b, j, i: (row_blk0 + b, j))],
        out_specs=pl.BlockSpec((tm, tn), lambda b, j, i: (b * mt + i, j)),
        out_shape=jax.ShapeDtypeStruct((n_batch * n, width), BF16),
        name="position_dft")(cn, sn, re, im)


def _route(logits, n_groups, per_group):
    glog = logits[:, :n_groups]
    g_hot = jax.nn.one_hot(jnp.argmax(glog, axis=-1), n_groups, dtype=F32)
    p_grp = jnp.sum(jax.nn.softmax(glog, axis=-1) * g_hot, axis=-1, keepdims=True)
    elog = logits[:, n_groups:n_groups + n_groups * per_group].reshape(-1, n_groups, per_group)
    elog_sel = jnp.sum(elog * g_hot[:, :, None], axis=1)
    i0 = jnp.argmax(elog_sel, axis=-1)
    hot0 = jax.nn.one_hot(i0, per_group, dtype=jnp.bool_)
    rest = jnp.where(hot0, -jnp.inf, elog_sel)
    i1 = jnp.argmax(rest, axis=-1)
    top_v = jnp.stack([jnp.max(elog_sel, axis=-1), jnp.max(rest, axis=-1)], axis=-1)
    w_top = jax.nn.softmax(top_v, axis=-1) * p_grp
    gsel = jnp.argmax(glog, axis=-1)
    eid = gsel[:, None] * per_group + jnp.stack([i0, i1], axis=-1)
    return eid.astype(jnp.int32), w_top


def _dispatch_plan(eid, n_experts, tm):
    n_slots = eid.shape[0] * TOP_K
    n_tiles = n_slots // tm + n_experts
    hot = (eid.reshape(-1)[:, None] == jnp.arange(n_experts, dtype=jnp.int32)[None, :])
    hot = hot.astype(jnp.int32)
    running = jnp.cumsum(hot, axis=0)
    counts = running[-1]
    padded = ((counts + tm - 1) // tm) * tm
    pad_end = jnp.cumsum(padded)
    pad_start = pad_end - padded
    slot_pos = jnp.sum(hot * (running - 1 + pad_start[None, :]), axis=1).astype(jnp.int32)
    tile_row0 = jnp.arange(n_tiles, dtype=jnp.int32) * tm
    tile_e = jnp.sum((tile_row0[:, None] >= pad_end[None, :]).astype(jnp.int32), axis=1)
    tile_e = jnp.minimum(tile_e, n_experts - 1).astype(jnp.int32)
    e_hot = (tile_e[:, None] == jnp.arange(n_experts, dtype=jnp.int32)[None, :]).astype(jnp.int32)
    real_end = jnp.sum(e_hot * (pad_start + counts)[None, :], axis=1)
    tile_used = jnp.clip(real_end - tile_row0, 0, tm).astype(jnp.int32)
    return slot_pos, tile_e, tile_used, n_tiles


def _dispatch(xp, slot_pos, *, n_rows_out):
    m, dh = xp.shape
    tr = _pick(m, 256, 128)

    def body(pos_ref, x_ref, init_hbm, o_hbm, sem):
        del init_hbm
        i = pl.program_id(0)

        def row_copy(r, k):
            dst = pos_ref[(i * tr + r) * TOP_K + k]
            return pltpu.make_async_copy(x_ref.at[pl.ds(r, 1), :], o_hbm.at[pl.ds(dst, 1), :],
                                         sem.at[0])

        def issue(r, carry):
            for k in range(TOP_K):
                row_copy(r, k).start()
            return carry

        lax.fori_loop(0, tr, issue, 0, unroll=8)
        for k in range(TOP_K):
            pltpu.make_async_copy(x_ref, o_hbm.at[pl.ds(0, tr), :], sem.at[0]).wait()

    return _pcall(
        body, grid=(m // tr,), num_scalar_prefetch=1,
        in_specs=[pl.BlockSpec((tr, dh), lambda i, pos: (i, 0)),
                  pl.BlockSpec(memory_space=pl.ANY)],
        out_specs=pl.BlockSpec(memory_space=pl.ANY),
        out_shape=jax.ShapeDtypeStruct((n_rows_out, dh), U32),
        scratch_shapes=[pltpu.SemaphoreType.DMA((1,))],
        input_output_aliases={2: 0},
        name="moe_dispatch")(slot_pos, xp, jnp.zeros((n_rows_out, dh), U32))


def _experts(xs, w1, w3, w2, tile_e, tile_used, *, n_tiles, tm):
    dh = xs.shape[1]
    D = 2 * dh
    fq = w1.shape[2] // FF_SPLITS
    cn = _pick(D, 512, 256, 128)

    def blk(i, q):
        return jnp.where(i % 2 == 0, q, FF_SPLITS - 1 - q)

    half_rows = tm // 2

    def body(te_ref, tu_ref, x_ref, w1_ref, w3_ref, w2_ref, o_ref, xlo_ref, xhi_ref, acc_ref):
        i, q = pl.program_id(0), pl.program_id(1)
        used = tu_ref[i]
        valid = used > 0
        small = used <= half_rows

        @pl.when(jnp.logical_and(valid, q == 0))
        def _():
            lo, hi = _unpack_halves(x_ref[...])
            xlo_ref[...] = lo.astype(BF16)
            xhi_ref[...] = hi.astype(BF16)

        def compute(nr):
            xlo, xhi = xlo_ref[:nr, :], xhi_ref[:nr, :]
            a = (jnp.dot(xlo, w1_ref[:dh, :], preferred_element_type=F32)
                 + jnp.dot(xhi, w1_ref[dh:, :], preferred_element_type=F32))
            b = (jnp.dot(xlo, w3_ref[:dh, :], preferred_element_type=F32)
                 + jnp.dot(xhi, w3_ref[dh:, :], preferred_element_type=F32))
            hid = (a * jax.nn.sigmoid(a) * b).astype(BF16)

            @pl.when(q == 0)
            def _():
                for c in range(0, D, cn):
                    acc_ref[:nr, c:c + cn] = jnp.dot(hid, w2_ref[:, c:c + cn],
                                                     preferred_element_type=F32)

            @pl.when(q > 0)
            def _():
                for c in range(0, D, cn):
                    acc_ref[:nr, c:c + cn] += jnp.dot(hid, w2_ref[:, c:c + cn],
                                                      preferred_element_type=F32)

            @pl.when(q == FF_SPLITS - 1)
            def _():
                o_ref[:nr, :] = _pack_halves(acc_ref[:nr, :])
                if nr < tm:
                    o_ref[nr:, :] = jnp.zeros((tm - nr, dh), o_ref.dtype)

        @pl.when(jnp.logical_and(valid, jnp.logical_not(small)))
        def _():
            compute(tm)

        @pl.when(jnp.logical_and(valid, small))
        def _():
            compute(half_rows)

        @pl.when(jnp.logical_and(jnp.logical_not(valid), q == FF_SPLITS - 1))
        def _():
            o_ref[...] = jnp.zeros(o_ref.shape, o_ref.dtype)

    return _pcall(
        body, grid=(n_tiles, FF_SPLITS), num_scalar_prefetch=2,
        in_specs=[pl.BlockSpec((tm, dh), lambda i, q, te, tv: (i, 0)),
                  pl.BlockSpec((None, D, fq), lambda i, q, te, tv: (te[i], 0, blk(i, q))),
                  pl.BlockSpec((None, D, fq), lambda i, q, te, tv: (te[i], 0, blk(i, q))),
                  pl.BlockSpec((None, fq, D), lambda i, q, te, tv: (te[i], blk(i, q), 0))],
        out_specs=pl.BlockSpec((tm, dh), lambda i, q, te, tv: (i, 0)),
        out_shape=jax.ShapeDtypeStruct((n_tiles * tm, dh), U32),
        scratch_shapes=[pltpu.VMEM((tm, dh), BF16), pltpu.VMEM((tm, dh), BF16),
                        pltpu.VMEM((tm, D), F32)],
        name="experts")(tile_e, tile_used, xs, w1, w3, w2)


def _combine(y_sorted, slot_pos, w_top, z, mods, gate_idx, final_g, *, m, seq, n_batch):
    D = z.shape[1]
    dh = D // 2
    tc = _pick(m, 256, 128)
    n_steps = m // tc

    def body(pos_ref, y_hbm, w_ref, z_ref, g_ref, fg_ref, o_ref, buf_ref, sem):
        i = pl.program_id(0)

        def row_copy(step, slot, r, k):
            src = pos_ref[(step * tc + r) * TOP_K + k]
            return pltpu.make_async_copy(y_hbm.at[pl.ds(src, 1), :],
                                         buf_ref.at[slot, k, pl.ds(r, 1), :], sem.at[slot])

        def start_gather(step, slot):
            def issue(r, carry):
                for k in range(TOP_K):
                    row_copy(step, slot, r, k).start()
                return carry
            lax.fori_loop(0, tc, issue, 0, unroll=8)

        def wait_gather(slot):
            for k in range(TOP_K):
                pltpu.make_async_copy(y_hbm.at[pl.ds(0, tc), :], buf_ref.at[slot, k],
                                      sem.at[slot]).wait()

        slot = i % 2

        @pl.when(i == 0)
        def _():
            start_gather(0, 0)

        @pl.when(i + 1 < n_steps)
        def _():
            start_gather(i + 1, 1 - slot)

        wait_gather(slot)
        wv = w_ref[...]
        y_lo = jnp.zeros((tc, dh), F32)
        y_hi = jnp.zeros((tc, dh), F32)
        for k in range(TOP_K):
            lo, hi = _unpack_halves(buf_ref[slot, k])
            y_lo = y_lo + wv[:, k:k + 1] * lo
            y_hi = y_hi + wv[:, k:k + 1] * hi
        z_lo = z_ref[:, :dh] + g_ref[:, :dh] * y_lo
        z_hi = z_ref[:, dh:] + g_ref[:, dh:] * y_hi
        if final_g is not None:
            ssq = (jnp.sum(z_lo * z_lo, axis=-1, keepdims=True)
                   + jnp.sum(z_hi * z_hi, axis=-1, keepdims=True))
            inv = lax.rsqrt(ssq / D + EPS)
            z_lo = z_lo * inv * fg_ref[:, :dh]
            z_hi = z_hi * inv * fg_ref[:, dh:]
        o_ref[:, :dh] = z_lo
        o_ref[:, dh:] = z_hi

    fg = jnp.ones((1, D), F32) if final_g is None else final_g
    return _pcall(
        body, grid=(n_steps,), num_scalar_prefetch=1,
        in_specs=[pl.BlockSpec(memory_space=pl.ANY),
                  pl.BlockSpec((tc, TOP_K), lambda i, pos: (i, 0)),
                  pl.BlockSpec((tc, D), lambda i, pos: (i, 0)),
                  _mod_spec(gate_idx, tc, D, seq, n_batch),
                  pl.BlockSpec((1, D), lambda i, pos: (0, 0))],
        out_specs=pl.BlockSpec((tc, D), lambda i, pos: (i, 0)),
        out_shape=jax.ShapeDtypeStruct((m, D), F32),
        scratch_shapes=[pltpu.VMEM((2, TOP_K, tc, dh), U32), pltpu.SemaphoreType.DMA((2,))],
        name="moe_combine")(slot_pos, y_sorted, w_top, z, mods, fg)


def _rope_tables(seq, head_dim, n_batch, n_ctx_rows):
    half = head_dim // 2
    inv = ROPE_THETA ** (-jnp.arange(0, half, 2, dtype=F32) / half)
    t = jnp.arange(seq)
    row_ang = (t // GRID_W).astype(F32)[:, None] * inv
    col_ang = (t % GRID_W).astype(F32)[:, None] * inv
    cr, sr, cc, sc = jnp.cos(row_ang), jnp.sin(row_ang), jnp.cos(col_ang), jnp.sin(col_ang)
    cos = jnp.concatenate([cr, cr, cc, cc], axis=-1)
    sin = jnp.concatenate([-sr, sr, -sc, sc], axis=-1)
    cos = jnp.concatenate([jnp.tile(cos, (n_batch, 1)), jnp.ones((n_ctx_rows, head_dim), F32)])
    sin = jnp.concatenate([jnp.tile(sin, (n_batch, 1)), jnp.zeros((n_ctx_rows, head_dim), F32)])
    return cos, sin


def kernel(x, c, ctx, c_ctx, w_mod, b_mod, norm1_g, norm2_g, w_in, b_gate, q_gain, k_gain,
           w_dw, b_dw, conv_ln_g, conv_ln_b, w_attn_o, w_conv_o, w_four_o, w_out,
           w_grp, b_grp, w_rt, b_rt, w1, w3, w2, final_g):
    n_batch, seq, D = x.shape
    ctx_len = ctx.shape[1]
    depth = w_mod.shape[0]
    hd = q_gain.shape[-1]
    q_w = w_attn_o.shape[1]
    conv_w = w_conv_o.shape[1]
    four_w = w_four_o.shape[1]
    kv_w = (w_in.shape[2] - q_w - 2 * conv_w - four_w - 3 * D) // 2
    n_kv = kv_w // hd
    grp = q_w // kv_w
    n_groups = w_grp.shape[2]
    n_experts = w_rt.shape[2]
    per_group = n_experts // n_groups
    o_k, o_v, o_a = q_w, q_w + kv_w, q_w + 2 * kv_w
    o_b, o_f, o_g = o_a + conv_w, o_a + 2 * conv_w, o_a + 2 * conv_w + four_w

    m_lat = n_batch * seq
    m_ctx = n_batch * ctx_len
    m_all = m_lat + m_ctx
    tm = _pick(m_ctx, 512, 256, 128)
    assert seq % tm == 0 and m_ctx % tm == 0 and seq % ctx_len == 0 and seq % GRID_W == 0
    n_vec = n_batch + 1
    assert n_vec <= 8

    z, z_tail = x.reshape(m_lat, D), ctx.reshape(m_ctx, D)
    cs_t = jnp.zeros((D, 8), F32).at[:, :n_batch].set(c.T).at[:, n_batch].set(c_ctx)
    cos_t, sin_t = _rope_tables(seq, hd, n_batch, m_ctx)
    cn_lat, sn_lat = _dft_tables(seq)
    cn_ctx, sn_ctx = _dft_tables(ctx_len)
    gw = four_w // FOUR_GROUPS
    chan_dft = _channel_dft_matrix(gw)
    seg = dict(seq=seq, n_batch=n_batch)
    plain = lambda dots: dots[0]

    w_in_b, w_out_b, w_attn_o_b = (w[0:1].astype(BF16) for w in (w_in, w_out, w_attn_o))
    w_conv_o_b, w_four_o_b = w_conv_o.astype(BF16), w_four_o.astype(BF16)
    ff = w1.shape[3]
    expert_srcs = [(w1.reshape(depth * n_experts * D, ff), n_experts * D),
                   (w3.reshape(depth * n_experts * D, ff), n_experts * D),
                   (w2.reshape(depth * n_experts * ff, D), n_experts * ff)]
    b_mod3 = b_mod[:, None, :]

    for l in range(depth):
        last = l == depth - 1
        m = m_lat if last else m_all
        mods = _ada(cs_t, w_mod, b_mod3, l, n_vec)[:n_vec].reshape(n_vec * N_MOD, 1, D)

        h = _norm_mod(z, z_tail, norm1_g[l][None, :], mods, 0, 1, m=m_all, **seg)

        tn_h = _pick(math.gcd(q_w, kv_w), 512, 256, 128)
        rope = [("rows", cos_t), ("rows", sin_t)]
        q = _matmul([h], [(w_in_b, 0, 0, 0)], [("full", q_gain[l][None, :])] + rope,
                    _head_norm_rope_epilogue(hd, hd ** -0.5 * math.log2(math.e)),
                    m=m, n=q_w, tm=tm, tn=tn_h, out_dtype=BF16, name="q_proj", row_split=4)
        k = _matmul([h], [(w_in_b, 0, 0, o_k // tn_h)], [("full", k_gain[l][None, :])] + rope,
                    _head_norm_rope_epilogue(hd, 1.0),
                    m=m_all, n=kv_w, tm=tm, tn=tn_h, out_dtype=BF16, name="k_proj", row_split=2)
        v = _matmul([h], [(w_in_b, 0, 0, o_v // tn_h)], [], plain,
                    m=m_all, n=kv_w, tm=tm, tn=tn_h, out_dtype=BF16, name="v_proj")
        geo = dict(n_batch=n_batch, seq=seq, ctx_len=ctx_len, n_kv=n_kv, grp=grp, hd=hd)
        srcs = [(a, rows, l) for a, rows in expert_srcs]
        if not last:
            srcs += [(w_in.reshape(depth * D, -1), D, l + 1), (w_out.reshape(depth * D, D), D, l + 1),
                     (w_attn_o.reshape(depth * q_w, D), q_w, l + 1)]
        att, casts = _attn_latent(q, k, v, srcs, **geo)
        w1_b = casts[0].reshape(n_experts, D, ff)
        w3_b = casts[1].reshape(n_experts, D, ff)
        w2_b = casts[2].reshape(n_experts, ff, D)
        if not last:
            att = jnp.concatenate([att, _attn_context(q, k, v, **geo)], axis=0)

        tn_c = _pick(math.gcd(conv_w, four_w, o_a), 512, 256, 128)
        u = _matmul([h], [(w_in_b, 0, 0, o_a // tn_c), (w_in_b, 0, 0, o_b // tn_c)], [],
                    lambda dots: dots[0] * jax.nn.sigmoid(dots[1]),
                    m=m, n=conv_w, tm=tm, tn=tn_c, out_dtype=F32, name="glu_proj")
        conv = _conv_ln_silu(u, w_dw[l], b_dw[l][None, :], conv_ln_g[l][None, :],
                             conv_ln_b[l][None, :], m=m, seq=seq, ctx_len=ctx_len,
                             n_lat_rows=m_lat)

        f_re, f_im = _four_proj_dft(h, w_in_b, 0, o_f, chan_dft, m=m, width=four_w, tm=tm)
        four = _position_dft(f_re, f_im, cn_lat, sn_lat, n=seq, row_blk0=0, n_batch=n_batch,
                             width=four_w, scale=float((seq * gw) ** -0.5))
        if not last:
            four_ctx = _position_dft(f_re, f_im, cn_ctx, sn_ctx, n=ctx_len, row_blk0=m_lat // ctx_len,
                                     n_batch=n_batch, width=four_w,
                                     scale=float((ctx_len * gw) ** -0.5))
            four = jnp.concatenate([four, four_ctx], axis=0)

        tn_m = _pick(D, 256, 128)
        d_t, g_t = D // tn_m, o_g // tn_m

        def merge(dots, b0, b1, b2):
            return (jax.nn.sigmoid(dots[0] + b0) * dots[1] + jax.nn.sigmoid(dots[2] + b1) * dots[3]
                    + jax.nn.sigmoid(dots[4] + b2) * dots[5])

        bg = b_gate[l][None, :]
        merged = _matmul(
            [h, att, conv, four],
            [(w_in_b, 0, 0, g_t), (w_attn_o_b, 0, 1, 0),
             (w_in_b, 0, 0, g_t + d_t), (w_conv_o_b, l, 2, 0),
             (w_in_b, 0, 0, g_t + 2 * d_t), (w_four_o_b, l, 3, 0)],
            [("col", bg, 0), ("col", bg, d_t), ("col", bg, 2 * d_t)], merge,
            m=m, n=D, tm=tm, tn=tn_m, out_dtype=BF16, name="gated_merge")

        tn_o = _pick(D, 1024, 512, 256, 128)
        resid = ("tile2", z, z_tail) if z_tail is not None and m > z.shape[0] else ("tile", z)
        z1 = _matmul([merged], [(w_out_b, 0, 0, 0)], [resid, ("mod", mods, 2)],
                     lambda dots, zt, g: zt + g * dots[0],
                     m=m, n=D, tm=tm, tn=tn_o, out_dtype=F32, name="out_proj", **seg)

        w_router = jnp.zeros((D, ROUTER_LANES), F32)
        w_router = w_router.at[:, :n_groups].set(w_grp[l]).at[:, n_groups:n_groups + n_experts].set(w_rt[l])
        b_router = jnp.zeros((1, ROUTER_LANES), F32)
        b_router = b_router.at[0, :n_groups].set(b_grp[l]).at[0, n_groups:n_groups + n_experts].set(b_rt[l])
        xp, logits = _norm_mod_router(z1, norm2_g[l][None, :], mods, 3, 4, w_router, b_router,
                                      m=m, **seg)
        eid, w_top = _route(logits, n_groups, per_group)
        tm_e = _pick(m * TOP_K, 512, 256, 128)
        slot_pos, tile_e, tile_used, n_tiles = _dispatch_plan(eid, n_experts, tm_e)
        xs = _dispatch(xp, slot_pos, n_rows_out=n_tiles * tm_e)
        y_sorted = _experts(xs, w1_b, w3_b, w2_b, tile_e, tile_used, n_tiles=n_tiles, tm=tm_e)
        z = _combine(y_sorted, slot_pos, w_top, z1, mods, 5,
                     final_g[None, :] if last else None, m=m, **seg)
        z_tail = None
        if not last:
            w_in_b, w_out_b, w_attn_o_b = (w[None] for w in casts[3:])

    return z.reshape(n_batch, seq, D)
```

```python
import math

import numpy as np
import jax
import jax.numpy as jnp
from jax import lax
from jax.experimental import pallas as pl
from jax.experimental.pallas import tpu as pltpu

F32 = jnp.float32
BF16 = jnp.bfloat16

EPS = 1e-6
GRID_W = 64
ROPE_THETA = 10000.0
TOP_K = 2
FOUR_GROUPS = 4
N_MOD = 6
CONV_HALO = 16
ROUTER_LANES = 128
FF_SPLITS = 4

V7X_VMEM_LIMIT_BYTES = 56 * 1024 * 1024


def _pcall(body, *, grid, in_specs, out_specs, out_shape, scratch_shapes=(), name=None,
           num_scalar_prefetch=0, input_output_aliases=None):
    params = pltpu.CompilerParams(dimension_semantics=("arbitrary",) * len(grid),
                                  vmem_limit_bytes=V7X_VMEM_LIMIT_BYTES)
    aliases = input_output_aliases or {}
    if num_scalar_prefetch:
        grid_spec = pltpu.PrefetchScalarGridSpec(
            num_scalar_prefetch=num_scalar_prefetch, grid=grid, in_specs=in_specs,
            out_specs=out_specs, scratch_shapes=scratch_shapes)
        return pl.pallas_call(body, grid_spec=grid_spec, out_shape=out_shape,
                              compiler_params=params, name=name, input_output_aliases=aliases)
    return pl.pallas_call(body, grid=grid, in_specs=in_specs, out_specs=out_specs,
                          out_shape=out_shape, scratch_shapes=scratch_shapes,
                          compiler_params=params, name=name, input_output_aliases=aliases)


def _pick(n, *cands):
    for c in cands:
        if n % c == 0:
            return c
    return n


def _ada(cs_t, w, b, layer, n_vec):
    _, D, N = w.shape
    tk = _pick(D, 512, 256, 128)
    tn = _pick(N, 2048, 1024, 512, 256, 128)

    def body(c_ref, w_ref, b_ref, o_ref):
        @pl.when(pl.program_id(1) == 0)
        def _():
            o_ref[...] = jnp.broadcast_to(b_ref[...], o_ref.shape)

        c = c_ref[...]
        s = c * jax.nn.sigmoid(c)
        wv = w_ref[...]
        rows = [jnp.sum(s[:, r:r + 1] * wv, axis=0, keepdims=True) for r in range(n_vec)]
        rows.append(jnp.zeros((8 - n_vec, tn), F32))
        o_ref[...] += jnp.concatenate(rows, axis=0)

    return _pcall(
        body, grid=(N // tn, D // tk),
        in_specs=[pl.BlockSpec((tk, 8), lambda j, k: (k, 0)),
                  pl.BlockSpec((None, tk, tn), lambda j, k: (layer, k, j)),
                  pl.BlockSpec((None, 1, tn), lambda j, k: (layer, 0, j))],
        out_specs=pl.BlockSpec((8, tn), lambda j, k: (0, j)),
        out_shape=jax.ShapeDtypeStruct((8, N), F32), name="ada")(cs_t, w, b)


def _seg_of_row(row0, seq, n_batch):
    return jnp.minimum(row0 // seq, n_batch)


def _mod_spec(idx, tr, width, seq, n_batch):
    return pl.BlockSpec((None, 1, width),
                        lambda i, *_: (_seg_of_row(i * tr, seq, n_batch) * N_MOD + idx, 0, 0))


def _norm_mod(x, x_tail, g, mods, sh_idx, sc_idx, *, m, seq, n_batch):
    D = x.shape[1]
    tr = _pick(m, 256, 128)
    head_tiles = x.shape[0] // tr
    if x_tail is None:
        x_tail = x

    def body(x_ref, t_ref, g_ref, sh_ref, sc_ref, o_ref):
        xf = jnp.where(pl.program_id(0) < head_tiles, x_ref[...], t_ref[...])
        y = xf * lax.rsqrt(jnp.mean(xf * xf, axis=-1, keepdims=True) + EPS) * g_ref[...]
        o_ref[...] = (y * (1.0 + sc_ref[...]) + sh_ref[...]).astype(o_ref.dtype)

    return _pcall(
        body, grid=(m // tr,),
        in_specs=[pl.BlockSpec((tr, D), lambda i: (jnp.minimum(i, head_tiles - 1), 0)),
                  pl.BlockSpec((tr, D), lambda i: (jnp.maximum(i - head_tiles, 0), 0)),
                  pl.BlockSpec((1, D), lambda i: (0, 0)),
                  _mod_spec(sh_idx, tr, D, seq, n_batch),
                  _mod_spec(sc_idx, tr, D, seq, n_batch)],
        out_specs=pl.BlockSpec((tr, D), lambda i: (i, 0)),
        out_shape=jax.ShapeDtypeStruct((m, D), BF16), name="norm_mod")(x, x_tail, g, mods, mods)


def _norm_mod_router(x, g, mods, sh_idx, sc_idx, w_router, b_router, *, m, seq, n_batch):
    D = x.shape[1]
    tr = _pick(m, 256, 128)

    w_hi = w_router.astype(BF16)
    w_lo = (w_router - w_hi.astype(F32)).astype(BF16)

    def body(x_ref, g_ref, sh_ref, sc_ref, whi_ref, wlo_ref, br_ref, o_ref, l_ref):
        xf = x_ref[...]
        y = xf * lax.rsqrt(jnp.mean(xf * xf, axis=-1, keepdims=True) + EPS) * g_ref[...]
        h = y * (1.0 + sc_ref[...]) + sh_ref[...]
        o_ref[...] = h
        h_hi = h.astype(BF16)
        h_lo = (h - h_hi.astype(F32)).astype(BF16)
        small = (jnp.dot(h_lo, whi_ref[...], preferred_element_type=F32)
                 + jnp.dot(h_hi, wlo_ref[...], preferred_element_type=F32))
        l_ref[...] = jnp.dot(h_hi, whi_ref[...], preferred_element_type=F32) + small + br_ref[...]

    return _pcall(
        body, grid=(m // tr,),
        in_specs=[pl.BlockSpec((tr, D), lambda i: (i, 0)),
                  pl.BlockSpec((1, D), lambda i: (0, 0)),
                  _mod_spec(sh_idx, tr, D, seq, n_batch),
                  _mod_spec(sc_idx, tr, D, seq, n_batch),
                  pl.BlockSpec((D, ROUTER_LANES), lambda i: (0, 0)),
                  pl.BlockSpec((D, ROUTER_LANES), lambda i: (0, 0)),
                  pl.BlockSpec((1, ROUTER_LANES), lambda i: (0, 0))],
        out_specs=[pl.BlockSpec((tr, D), lambda i: (i, 0)),
                   pl.BlockSpec((tr, ROUTER_LANES), lambda i: (i, 0))],
        out_shape=[jax.ShapeDtypeStruct((m, D), F32),
                   jax.ShapeDtypeStruct((m, ROUTER_LANES), F32)],
        name="norm_mod_router")(x, g, mods, mods, w_hi, w_lo, b_router)


def _matmul(xs, ws, extras, epilogue, *, m, n, tm, tn, out_dtype, name, seq=1, n_batch=0,
            row_split=1):
    in_specs, args = [], []
    for x in xs:
        in_specs.append(pl.BlockSpec((tm, x.shape[1]), lambda j, i: (i, 0)))
        args.append(x)
    for w, layer, _, off in ws:
        in_specs.append(pl.BlockSpec((None, w.shape[1], tn),
                                     lambda j, i, layer=layer, off=off: (layer, 0, j + off)))
        args.append(w)
    for ex in extras:
        kind, arr = ex[0], ex[1]
        if kind == "col":
            in_specs.append(pl.BlockSpec((1, tn), lambda j, i, off=ex[2]: (0, j + off)))
        elif kind == "tile":
            in_specs.append(pl.BlockSpec((tm, tn), lambda j, i: (i, j)))
        elif kind == "tile2":
            head_tiles = arr.shape[0] // tm
            in_specs.append(pl.BlockSpec(
                (tm, tn), lambda j, i, ht=head_tiles: (jnp.minimum(i, ht - 1), j)))
            args.append(arr)
            in_specs.append(pl.BlockSpec(
                (tm, tn), lambda j, i, ht=head_tiles: (jnp.maximum(i - ht, 0), j)))
            arr = ex[2]
        elif kind == "rows":
            in_specs.append(pl.BlockSpec((tm, arr.shape[1]), lambda j, i: (i, 0)))
        elif kind == "full":
            in_specs.append(pl.BlockSpec(arr.shape, lambda j, i: (0, 0)))
        elif kind == "mod":
            in_specs.append(pl.BlockSpec(
                (None, 1, tn),
                lambda j, i, idx=ex[2]: (_seg_of_row(i * tm, seq, n_batch) * N_MOD + idx, 0, j)))
        args.append(arr)
    nx, nw = len(xs), len(ws)
    sub = tm // row_split

    def body(*refs):
        x_refs, w_refs = refs[:nx], refs[nx:nx + nw]
        e_refs, o_ref = list(refs[nx + nw:-1]), refs[-1]
        i = pl.program_id(1)
        for r0 in range(0, tm, sub):
            rows = slice(r0, r0 + sub)
            dots = [jnp.dot(x_refs[xi][rows, :], w_refs[p][...], preferred_element_type=F32)
                    for p, (_, _, xi, _) in enumerate(ws)]
            vals, e = [], 0
            for ex in extras:
                if ex[0] == "tile2":
                    in_head = i < ex[1].shape[0] // tm
                    vals.append(jnp.where(in_head, e_refs[e][rows, :], e_refs[e + 1][rows, :]))
                    e += 2
                    continue
                vals.append(e_refs[e][rows, :] if ex[0] in ("tile", "rows") else e_refs[e][...])
                e += 1
            o_ref[rows, :] = epilogue(dots, *vals).astype(o_ref.dtype)

    return _pcall(body, grid=(n // tn, m // tm), in_specs=in_specs,
                  out_specs=pl.BlockSpec((tm, tn), lambda j, i: (i, j)),
                  out_shape=jax.ShapeDtypeStruct((m, n), out_dtype), name=name)(*args)


def _head_norm_rope_epilogue(head_dim, scale):
    quarter = head_dim // 4

    def epilogue(dots, gain, cos, sin_signed):
        d = dots[0]
        lane = lax.broadcasted_iota(jnp.int32, (d.shape[0], head_dim), 1)
        first_half = (lane % (2 * quarter)) < quarter
        outs = []
        for h in range(d.shape[1] // head_dim):
            xh = d[:, h * head_dim:(h + 1) * head_dim]
            y = xh * lax.rsqrt(jnp.mean(xh * xh, axis=-1, keepdims=True) + EPS) * gain
            partner = jnp.where(first_half, pltpu.roll(y, head_dim - quarter, 1),
                                pltpu.roll(y, quarter, 1))
            outs.append((y * cos + partner * sin_signed) * scale)
        return jnp.concatenate(outs, axis=-1)

    return epilogue


def _softmax_pv(q, keys, values):
    dn = (((1,), (1,)), ((), ()))
    s = [lax.dot_general(q, k, dn, preferred_element_type=F32) for k in keys]
    m = s[0].max(axis=-1, keepdims=True)
    for si in s[1:]:
        m = jnp.maximum(m, si.max(axis=-1, keepdims=True))
    p = [jnp.exp2(si - m) for si in s]
    l = p[0].sum(axis=-1, keepdims=True)
    for pi in p[1:]:
        l = l + pi.sum(axis=-1, keepdims=True)
    o = jnp.dot(p[0].astype(BF16), values[0], preferred_element_type=F32)
    for pi, v in zip(p[1:], values[1:]):
        o = o + jnp.dot(pi.astype(BF16), v, preferred_element_type=F32)
    return o / l


def _attn_latent(q, k, v, cast_srcs, *, n_batch, seq, ctx_len, n_kv, grp, hd):
    tq = _pick(seq, 256, 128)
    ck = _pick(seq, 256, 128)
    qt = seq // tq
    ctx_blk0 = n_batch * seq // ctx_len
    n_steps = n_batch * n_kv * qt
    n_cast = len(cast_srcs)
    dn = (((1,), (1,)), ((), ()))

    def body(q_ref, kl_ref, vl_ref, kc_ref, vc_ref, *rest):
        src_refs, o_ref, dst_refs = rest[:n_cast], rest[n_cast], rest[n_cast + 1:]
        for s_ref, d_ref in zip(src_refs, dst_refs):
            d_ref[...] = s_ref[...].astype(BF16)

        qs = jnp.concatenate([q_ref[:, g * hd:(g + 1) * hd] for g in range(grp)], axis=0)
        chunks = [(kc_ref, vc_ref, 0, ctx_len)]
        chunks += [(kl_ref, vl_ref, c0, ck) for c0 in range(0, seq, ck)]
        m = l = acc = None
        for kr, vr, c0, n in chunks:
            s = lax.dot_general(qs, kr[c0:c0 + n, :], dn, preferred_element_type=F32)
            mc = s.max(axis=-1, keepdims=True)
            if m is None:
                m_new = mc
                p = jnp.exp2(s - m_new)
                l = p.sum(axis=-1, keepdims=True)
                acc = jnp.dot(p.astype(BF16), vr[c0:c0 + n, :], preferred_element_type=F32)
            else:
                m_new = jnp.maximum(m, mc)
                alpha = jnp.exp2(m - m_new)
                p = jnp.exp2(s - m_new)
                l = alpha * l + p.sum(axis=-1, keepdims=True)
                acc = alpha * acc + jnp.dot(p.astype(BF16), vr[c0:c0 + n, :],
                                            preferred_element_type=F32)
            m = m_new
        o = acc / l
        for g in range(grp):
            o_ref[:, g * hd:(g + 1) * hd] = o[g * tq:(g + 1) * tq, :].astype(o_ref.dtype)

    def step(b, h, i):
        return (b * n_kv + h) * qt + i

    lat_spec = pl.BlockSpec((seq, hd), lambda b, h, i: (b, h))
    ctx_spec = pl.BlockSpec((ctx_len, hd), lambda b, h, i: (ctx_blk0 + b, h))
    cast_in, cast_out, cast_shapes = [], [], []
    for w, rows, layer in cast_srcs:
        assert rows % (16 * n_steps) == 0, (rows, n_steps)
        blk = (rows // n_steps, w.shape[1])
        cast_in.append(pl.BlockSpec(
            blk, lambda b, h, i, layer=layer: (layer * n_steps + step(b, h, i), 0)))
        cast_out.append(pl.BlockSpec(blk, lambda b, h, i: (step(b, h, i), 0)))
        cast_shapes.append(jax.ShapeDtypeStruct((rows, w.shape[1]), BF16))
    q_spec = pl.BlockSpec((tq, grp * hd), lambda b, h, i: (b * qt + i, h))
    outs = _pcall(
        body, grid=(n_batch, n_kv, qt),
        in_specs=[q_spec, lat_spec, lat_spec, ctx_spec, ctx_spec] + cast_in,
        out_specs=[q_spec] + cast_out,
        out_shape=[jax.ShapeDtypeStruct((n_batch * seq, n_kv * grp * hd), BF16)] + cast_shapes,
        name="attn_latent")(q, k, v, k, v, *[w for w, _, _ in cast_srcs])
    return outs[0], outs[1:]


def _attn_context(q, k, v, *, n_batch, seq, ctx_len, n_kv, grp, hd):
    ctx_blk0 = n_batch * seq // ctx_len

    def body(q_ref, kc_ref, vc_ref, o_ref):
        keys, values = (kc_ref[...],), (vc_ref[...],)
        for g in range(grp):
            o = _softmax_pv(q_ref[:, g * hd:(g + 1) * hd], keys, values)
            o_ref[:, g * hd:(g + 1) * hd] = o.astype(o_ref.dtype)

    ctx_spec = pl.BlockSpec((ctx_len, hd), lambda b, h: (ctx_blk0 + b, h))
    return _pcall(
        body, grid=(n_batch, n_kv),
        in_specs=[pl.BlockSpec((ctx_len, grp * hd), lambda b, h: (ctx_blk0 + b, h)),
                  ctx_spec, ctx_spec],
        out_specs=pl.BlockSpec((ctx_len, grp * hd), lambda b, h: (b, h)),
        out_shape=jax.ShapeDtypeStruct((n_batch * ctx_len, n_kv * grp * hd), BF16),
        name="attn_context")(q, k, v)


def _conv_ln_silu(u, w_dw, b_dw, ln_g, ln_b, *, m, seq, ctx_len, n_lat_rows):
    cw = u.shape[1]
    taps = w_dw.shape[0]
    half = taps // 2
    ts = ctx_len
    halo_per_tile = ts // CONV_HALO
    n_halo_blocks = m // CONV_HALO
    lat_tiles = n_lat_rows // ts
    tiles_per_seq = seq // ts
    rc = 16
    conv_rows = _pick(ts, 128, 64, 32)
    conv_cols = _pick(cw, 128)
    assert CONV_HALO % 8 == 0 and half <= CONV_HALO

    def body(u_ref, prev_ref, next_ref, w_ref, b_ref, g_ref, beta_ref, o_ref, win_ref, y_ref):
        t = pl.program_id(0)
        in_lat = t < lat_tiles
        pos = t % tiles_per_seq
        is_first = jnp.logical_or(jnp.logical_not(in_lat), pos == 0)
        is_last = jnp.logical_or(jnp.logical_not(in_lat), pos == tiles_per_seq - 1)
        win_ref[0:CONV_HALO, :] = jnp.where(is_first, 0.0, prev_ref[...])
        win_ref[CONV_HALO:CONV_HALO + ts, :] = u_ref[...]
        win_ref[CONV_HALO + ts:, :] = jnp.where(is_last, 0.0, next_ref[...])
        for c0 in range(0, ts, conv_rows):
            for j0 in range(0, cw, conv_cols):
                acc = None
                for rho in range(8):
                    part = None
                    n_rows = conv_rows if rho == 0 else conv_rows + 8
                    for kk in range(taps):
                        d = CONV_HALO - half + kk
                        if d % 8 != rho:
                            continue
                        r = c0 + d - rho
                        term = win_ref[r:r + n_rows, j0:j0 + conv_cols] * w_ref[kk:kk + 1, j0:j0 + conv_cols]
                        part = term if part is None else part + term
                    if part is None:
                        continue
                    part = part[rho:rho + conv_rows, :]
                    acc = part if acc is None else acc + part
                y_ref[c0:c0 + conv_rows, j0:j0 + conv_cols] = acc + b_ref[:, j0:j0 + conv_cols]
        for c0 in range(0, ts, rc):
            y = y_ref[c0:c0 + rc, :]
            mu = jnp.mean(y, axis=-1, keepdims=True)
            yc = y - mu
            var = jnp.mean(yc * yc, axis=-1, keepdims=True)
            z = yc * lax.rsqrt(var + EPS) * g_ref[...] + beta_ref[...]
            o_ref[c0:c0 + rc, :] = (z * jax.nn.sigmoid(z)).astype(o_ref.dtype)

    vec = pl.BlockSpec((1, cw), lambda t: (0, 0))
    return _pcall(
        body, grid=(m // ts,),
        in_specs=[pl.BlockSpec((ts, cw), lambda t: (t, 0)),
                  pl.BlockSpec((CONV_HALO, cw),
                               lambda t: (jnp.maximum(t * halo_per_tile - 1, 0), 0)),
                  pl.BlockSpec((CONV_HALO, cw),
                               lambda t: (jnp.minimum((t + 1) * halo_per_tile, n_halo_blocks - 1), 0)),
                  pl.BlockSpec((taps, cw), lambda t: (0, 0)), vec, vec, vec],
        out_specs=pl.BlockSpec((ts, cw), lambda t: (t, 0)),
        out_shape=jax.ShapeDtypeStruct((m, cw), BF16),
        scratch_shapes=[pltpu.VMEM((ts + 2 * CONV_HALO, cw), F32), pltpu.VMEM((ts, cw), F32)],
        name="conv_ln_silu")(u, u, u, w_dw, b_dw, ln_g, ln_b)


def _dft_tables(n):
    j = np.arange(n, dtype=np.int64)[:, None]
    if n % 64 or n <= 64:
        ang = 2.0 * np.pi * ((j * j.T) % n) / n
        return jnp.asarray(np.cos(ang), BF16), jnp.asarray(np.sin(ang), BF16)
    lo = 64
    hi = n // lo
    a_hi = 2.0 * np.pi * ((j * lo * np.arange(hi, dtype=np.int64)[None, :]) % n) / n
    a_lo = 2.0 * np.pi * ((j * np.arange(lo, dtype=np.int64)[None, :]) % n) / n
    ch, sh = jnp.asarray(np.cos(a_hi), F32)[:, :, None], jnp.asarray(np.sin(a_hi), F32)[:, :, None]
    cl, sl = jnp.asarray(np.cos(a_lo), F32)[:, None, :], jnp.asarray(np.sin(a_lo), F32)[:, None, :]
    c = (ch * cl - sh * sl).reshape(n, n).astype(BF16)
    s = (sh * cl + ch * sl).reshape(n, n).astype(BF16)
    return c, s


def _channel_dft_matrix(gw):
    k = np.arange(gw, dtype=np.int64)
    ang = 2.0 * np.pi * ((k[:, None] * k[None, :]) % gw) / gw
    return jnp.asarray(np.concatenate([np.cos(ang), -np.sin(ang)], axis=1), F32).astype(BF16)


def _four_proj_dft(h, w, layer, col_off, chan_dft, *, m, width, tm):
    gw = chan_dft.shape[0]
    tn = _pick(width, 2 * gw, gw)
    sub = tm // 2

    def body(x_ref, w_ref, d_ref, re_ref, im_ref):
        for r0 in range(0, tm, sub):
            rows = slice(r0, r0 + sub)
            f = jnp.dot(x_ref[rows, :], w_ref[...], preferred_element_type=F32).astype(BF16)
            for g0 in range(0, tn, gw):
                zc = jnp.dot(f[:, g0:g0 + gw], d_ref[...], preferred_element_type=F32)
                re_ref[rows, g0:g0 + gw] = zc[:, :gw].astype(BF16)
                im_ref[rows, g0:g0 + gw] = zc[:, gw:].astype(BF16)

    out_spec = pl.BlockSpec((tm, tn), lambda j, i: (i, j))
    return _pcall(
        body, grid=(width // tn, m // tm),
        in_specs=[pl.BlockSpec((tm, h.shape[1]), lambda j, i: (i, 0)),
                  pl.BlockSpec((None, w.shape[1], tn), lambda j, i: (layer, 0, j + col_off // tn)),
                  pl.BlockSpec(chan_dft.shape, lambda j, i: (0, 0))],
        out_specs=[out_spec, out_spec],
        out_shape=[jax.ShapeDtypeStruct((m, width), BF16)] * 2,
        name="four_proj_dft")(h, w, chan_dft)


def _position_dft(re, im, cn, sn, *, n, row_blk0, n_batch, width, scale):
    tm = _pick(n, 512, 256, 128)
    tn = _pick(width, 512, 256, 128)
    mt = n // tm

    def body(c_ref, s_ref, re_ref, im_ref, o_ref):
        acc = jnp.dot(c_ref[...], re_ref[...], preferred_element_type=F32)
        acc = acc + jnp.dot(s_ref[...], im_ref[...], preferred_element_type=F32)
        o_ref[...] = (acc * scale).astype(o_ref.dtype)

    mat_spec = pl.BlockSpec((tm, n), lambda b, j, i: (i, 0))
    return _pcall(
        body, grid=(n_batch, width // tn, mt),
        in_specs=[mat_spec, mat_spec,
                  pl.BlockSpec((n, tn), lambda b, j, i: (row_blk0 + b, j)),
                  pl.BlockSpec((n, tn), lambda b, j, i: (row_blk0 + b, j))],
        out_specs=pl.BlockSpec((tm, tn), lambda b, j, i: (b * mt + i, j)),
        out_shape=jax.ShapeDtypeStruct((n_batch * n, width), BF16),
        name="position_dft")(cn, sn, re, im)


def _route(logits, n_groups, per_group):
    glog = logits[:, :n_groups]
    g_hot = jax.nn.one_hot(jnp.argmax(glog, axis=-1), n_groups, dtype=F32)
    p_grp = jnp.sum(jax.nn.softmax(glog, axis=-1) * g_hot, axis=-1, keepdims=True)
    elog = logits[:, n_groups:n_groups + n_groups * per_group].reshape(-1, n_groups, per_group)
    elog_sel = jnp.sum(elog * g_hot[:, :, None], axis=1)
    i0 = jnp.argmax(elog_sel, axis=-1)
    hot0 = jax.nn.one_hot(i0, per_group, dtype=jnp.bool_)
    rest = jnp.where(hot0, -jnp.inf, elog_sel)
    i1 = jnp.argmax(rest, axis=-1)
    top_v = jnp.stack([jnp.max(elog_sel, axis=-1), jnp.max(rest, axis=-1)], axis=-1)
    w_top = jax.nn.softmax(top_v, axis=-1) * p_grp
    gsel = jnp.argmax(glog, axis=-1)
    eid = gsel[:, None] * per_group + jnp.stack([i0, i1], axis=-1)
    return eid.astype(jnp.int32), w_top


def _dispatch_plan(eid, n_experts, tm):
    n_slots = eid.shape[0] * TOP_K
    n_tiles = n_slots // tm + n_experts
    hot = (eid.reshape(-1)[:, None] == jnp.arange(n_experts, dtype=jnp.int32)[None, :])
    hot = hot.astype(jnp.int32)
    running = jnp.cumsum(hot, axis=0)
    counts = running[-1]
    padded = ((counts + tm - 1) // tm) * tm
    pad_end = jnp.cumsum(padded)
    pad_start = pad_end - padded
    slot_pos = jnp.sum(hot * (running - 1 + pad_start[None, :]), axis=1).astype(jnp.int32)
    tile_row0 = jnp.arange(n_tiles, dtype=jnp.int32) * tm
    tile_e = jnp.sum((tile_row0[:, None] >= pad_end[None, :]).astype(jnp.int32), axis=1)
    tile_e = jnp.minimum(tile_e, n_experts - 1).astype(jnp.int32)
    e_hot = (tile_e[:, None] == jnp.arange(n_experts, dtype=jnp.int32)[None, :]).astype(jnp.int32)
    real_end = jnp.sum(e_hot * (pad_start + counts)[None, :], axis=1)
    tile_used = jnp.clip(real_end - tile_row0, 0, tm).astype(jnp.int32)
    return slot_pos, tile_e, tile_used, n_tiles


def _dispatch(xp, slot_pos, *, n_rows_out):
    m, dh = xp.shape
    tr = _pick(m, 256, 128)

    def body(pos_ref, x_ref, init_hbm, o_hbm, sem):
        del init_hbm
        i = pl.program_id(0)

        def row_copy(r, k):
            dst = pos_ref[(i * tr + r) * TOP_K + k]
            return pltpu.make_async_copy(x_ref.at[pl.ds(r, 1), :], o_hbm.at[pl.ds(dst, 1), :],
                                         sem.at[0])

        def issue(r, carry):
            for k in range(TOP_K):
                row_copy(r, k).start()
            return carry

        lax.fori_loop(0, tr, issue, 0, unroll=8)
        for k in range(TOP_K):
            pltpu.make_async_copy(x_ref, o_hbm.at[pl.ds(0, tr), :], sem.at[0]).wait()

    return _pcall(
        body, grid=(m // tr,), num_scalar_prefetch=1,
        in_specs=[pl.BlockSpec((tr, dh), lambda i, pos: (i, 0)),
                  pl.BlockSpec(memory_space=pl.ANY)],
        out_specs=pl.BlockSpec(memory_space=pl.ANY),
        out_shape=jax.ShapeDtypeStruct((n_rows_out, dh), xp.dtype),
        scratch_shapes=[pltpu.SemaphoreType.DMA((1,))],
        input_output_aliases={2: 0},
        name="moe_dispatch")(slot_pos, xp, jnp.zeros((n_rows_out, dh), xp.dtype))


def _experts(xs, w1, w3, w2, tile_e, tile_used, *, n_tiles, tm):
    D = xs.shape[1]
    fq = w1.shape[2] // FF_SPLITS
    cn = _pick(D, 512, 256, 128)

    def blk(i, q):
        return jnp.where(i % 2 == 0, q, FF_SPLITS - 1 - q)

    row_step = tm // 4 if tm % 64 == 0 else tm

    def body(te_ref, tu_ref, x_ref, w1_ref, w3_ref, w2_ref, o_ref, xb_ref):
        i, q = pl.program_id(0), pl.program_id(1)
        used = tu_ref[i]
        valid = used > 0

        @pl.when(jnp.logical_and(valid, q == 0))
        def _():
            xb_ref[...] = x_ref[...].astype(BF16)

        def compute(nr):
            x = xb_ref[:nr, :]
            a = jnp.dot(x, w1_ref[...], preferred_element_type=F32)
            b = jnp.dot(x, w3_ref[...], preferred_element_type=F32)
            hid = (a * jax.nn.sigmoid(a) * b).astype(BF16)

            @pl.when(q == 0)
            def _():
                for c in range(0, D, cn):
                    o_ref[:nr, c:c + cn] = jnp.dot(hid, w2_ref[:, c:c + cn],
                                                   preferred_element_type=F32)
                if nr < tm:
                    o_ref[nr:, :] = jnp.zeros((tm - nr, D), o_ref.dtype)

            @pl.when(q > 0)
            def _():
                for c in range(0, D, cn):
                    o_ref[:nr, c:c + cn] += jnp.dot(hid, w2_ref[:, c:c + cn],
                                                    preferred_element_type=F32)

        for nr in range(row_step, tm + 1, row_step):
            @pl.when(jnp.logical_and(used > nr - row_step, used <= nr))
            def _(nr=nr):
                compute(nr)

        @pl.when(jnp.logical_and(jnp.logical_not(valid), q == 0))
        def _():
            o_ref[...] = jnp.zeros(o_ref.shape, o_ref.dtype)

    return _pcall(
        body, grid=(n_tiles, FF_SPLITS), num_scalar_prefetch=2,
        in_specs=[pl.BlockSpec((tm, D), lambda i, q, te, tv: (i, 0)),
                  pl.BlockSpec((None, D, fq), lambda i, q, te, tv: (te[i], 0, blk(i, q))),
                  pl.BlockSpec((None, D, fq), lambda i, q, te, tv: (te[i], 0, blk(i, q))),
                  pl.BlockSpec((None, fq, D), lambda i, q, te, tv: (te[i], blk(i, q), 0))],
        out_specs=pl.BlockSpec((tm, D), lambda i, q, te, tv: (i, 0)),
        out_shape=jax.ShapeDtypeStruct((n_tiles * tm, D), F32),
        scratch_shapes=[pltpu.VMEM((tm, D), BF16)],
        name="experts")(tile_e, tile_used, xs, w1, w3, w2)


def _combine(y_sorted, slot_pos, w_top, z, mods, gate_idx, final_g, *, m, seq, n_batch):
    D = z.shape[1]
    tc = _pick(m, 256, 128)
    n_steps = m // tc

    def body(pos_ref, y_hbm, w_ref, z_ref, g_ref, fg_ref, o_ref, buf_ref, sem):
        i = pl.program_id(0)

        def row_copy(step, slot, r, k):
            src = pos_ref[(step * tc + r) * TOP_K + k]
            return pltpu.make_async_copy(y_hbm.at[pl.ds(src, 1), :],
                                         buf_ref.at[slot, k, pl.ds(r, 1), :], sem.at[slot])

        def start_gather(step, slot):
            def issue(r, carry):
                for k in range(TOP_K):
                    row_copy(step, slot, r, k).start()
                return carry
            lax.fori_loop(0, tc, issue, 0, unroll=8)

        def wait_gather(slot):
            for k in range(TOP_K):
                pltpu.make_async_copy(y_hbm.at[pl.ds(0, tc), :], buf_ref.at[slot, k],
                                      sem.at[slot]).wait()

        slot = i % 2

        @pl.when(i == 0)
        def _():
            start_gather(0, 0)

        @pl.when(i + 1 < n_steps)
        def _():
            start_gather(i + 1, 1 - slot)

        wait_gather(slot)
        wv = w_ref[...]
        y = wv[:, 0:1] * buf_ref[slot, 0]
        for k in range(1, TOP_K):
            y = y + wv[:, k:k + 1] * buf_ref[slot, k]
        zn = z_ref[...] + g_ref[...] * y
        if final_g is not None:
            zn = zn * lax.rsqrt(jnp.mean(zn * zn, axis=-1, keepdims=True) + EPS) * fg_ref[...]
        o_ref[...] = zn

    fg = jnp.ones((1, D), F32) if final_g is None else final_g
    return _pcall(
        body, grid=(n_steps,), num_scalar_prefetch=1,
        in_specs=[pl.BlockSpec(memory_space=pl.ANY),
                  pl.BlockSpec((tc, TOP_K), lambda i, pos: (i, 0)),
                  pl.BlockSpec((tc, D), lambda i, pos: (i, 0)),
                  _mod_spec(gate_idx, tc, D, seq, n_batch),
                  pl.BlockSpec((1, D), lambda i, pos: (0, 0))],
        out_specs=pl.BlockSpec((tc, D), lambda i, pos: (i, 0)),
        out_shape=jax.ShapeDtypeStruct((m, D), F32),
        scratch_shapes=[pltpu.VMEM((2, TOP_K, tc, D), F32), pltpu.SemaphoreType.DMA((2,))],
        name="moe_combine")(slot_pos, y_sorted, w_top, z, mods, fg)


def _rope_tables(seq, head_dim, n_batch, n_ctx_rows):
    half = head_dim // 2
    inv = ROPE_THETA ** (-jnp.arange(0, half, 2, dtype=F32) / half)
    t = jnp.arange(seq)
    row_ang = (t // GRID_W).astype(F32)[:, None] * inv
    col_ang = (t % GRID_W).astype(F32)[:, None] * inv
    cr, sr, cc, sc = jnp.cos(row_ang), jnp.sin(row_ang), jnp.cos(col_ang), jnp.sin(col_ang)
    cos = jnp.concatenate([cr, cr, cc, cc], axis=-1)
    sin = jnp.concatenate([-sr, sr, -sc, sc], axis=-1)
    cos = jnp.concatenate([jnp.tile(cos, (n_batch, 1)), jnp.ones((n_ctx_rows, head_dim), F32)])
    sin = jnp.concatenate([jnp.tile(sin, (n_batch, 1)), jnp.zeros((n_ctx_rows, head_dim), F32)])
    return cos, sin


def kernel(x, c, ctx, c_ctx, w_mod, b_mod, norm1_g, norm2_g, w_in, b_gate, q_gain, k_gain,
           w_dw, b_dw, conv_ln_g, conv_ln_b, w_attn_o, w_conv_o, w_four_o, w_out,
           w_grp, b_grp, w_rt, b_rt, w1, w3, w2, final_g):
    n_batch, seq, D = x.shape
    ctx_len = ctx.shape[1]
    depth = w_mod.shape[0]
    hd = q_gain.shape[-1]
    q_w = w_attn_o.shape[1]
    conv_w = w_conv_o.shape[1]
    four_w = w_four_o.shape[1]
    kv_w = (w_in.shape[2] - q_w - 2 * conv_w - four_w - 3 * D) // 2
    n_kv = kv_w // hd
    grp = q_w // kv_w
    n_groups = w_grp.shape[2]
    n_experts = w_rt.shape[2]
    per_group = n_experts // n_groups
    o_k, o_v, o_a = q_w, q_w + kv_w, q_w + 2 * kv_w
    o_b, o_f, o_g = o_a + conv_w, o_a + 2 * conv_w, o_a + 2 * conv_w + four_w

    m_lat = n_batch * seq
    m_ctx = n_batch * ctx_len
    m_all = m_lat + m_ctx
    tm = _pick(m_ctx, 512, 256, 128)
    assert seq % tm == 0 and m_ctx % tm == 0 and seq % ctx_len == 0 and seq % GRID_W == 0
    n_vec = n_batch + 1
    assert n_vec <= 8

    z, z_tail = x.reshape(m_lat, D), ctx.reshape(m_ctx, D)
    cs_t = jnp.zeros((D, 8), F32).at[:, :n_batch].set(c.T).at[:, n_batch].set(c_ctx)
    cos_t, sin_t = _rope_tables(seq, hd, n_batch, m_ctx)
    cn_lat, sn_lat = _dft_tables(seq)
    cn_ctx, sn_ctx = _dft_tables(ctx_len)
    gw = four_w // FOUR_GROUPS
    chan_dft = _channel_dft_matrix(gw)
    seg = dict(seq=seq, n_batch=n_batch)
    plain = lambda dots: dots[0]

    w_in_b, w_out_b, w_attn_o_b = (w[0:1].astype(BF16) for w in (w_in, w_out, w_attn_o))
    w_conv_o_b, w_four_o_b = w_conv_o.astype(BF16), w_four_o.astype(BF16)
    ff = w1.shape[3]
    expert_srcs = [(w1.reshape(depth * n_experts * D, ff), n_experts * D),
                   (w3.reshape(depth * n_experts * D, ff), n_experts * D),
                   (w2.reshape(depth * n_experts * ff, D), n_experts * ff)]
    b_mod3 = b_mod[:, None, :]

    for l in range(depth):
        last = l == depth - 1
        m = m_lat if last else m_all
        mods = _ada(cs_t, w_mod, b_mod3, l, n_vec)[:n_vec].reshape(n_vec * N_MOD, 1, D)

        h = _norm_mod(z, z_tail, norm1_g[l][None, :], mods, 0, 1, m=m_all, **seg)

        tn_h = _pick(math.gcd(q_w, kv_w), 512, 256, 128)
        rope = [("rows", cos_t), ("rows", sin_t)]
        q = _matmul([h], [(w_in_b, 0, 0, 0)], [("full", q_gain[l][None, :])] + rope,
                    _head_norm_rope_epilogue(hd, hd ** -0.5 * math.log2(math.e)),
                    m=m, n=q_w, tm=tm, tn=tn_h, out_dtype=BF16, name="q_proj", row_split=4)
        k = _matmul([h], [(w_in_b, 0, 0, o_k // tn_h)], [("full", k_gain[l][None, :])] + rope,
                    _head_norm_rope_epilogue(hd, 1.0),
                    m=m_all, n=kv_w, tm=tm, tn=tn_h, out_dtype=BF16, name="k_proj", row_split=4)
        v = _matmul([h], [(w_in_b, 0, 0, o_v // tn_h)], [], plain,
                    m=m_all, n=kv_w, tm=tm, tn=tn_h, out_dtype=BF16, name="v_proj")
        geo = dict(n_batch=n_batch, seq=seq, ctx_len=ctx_len, n_kv=n_kv, grp=grp, hd=hd)
        srcs = [(a, rows, l) for a, rows in expert_srcs]
        if not last:
            srcs += [(w_in.reshape(depth * D, -1), D, l + 1), (w_out.reshape(depth * D, D), D, l + 1),
                     (w_attn_o.reshape(depth * q_w, D), q_w, l + 1)]
        att, casts = _attn_latent(q, k, v, srcs, **geo)
        w1_b = casts[0].reshape(n_experts, D, ff)
        w3_b = casts[1].reshape(n_experts, D, ff)
        w2_b = casts[2].reshape(n_experts, ff, D)
        if not last:
            att = jnp.concatenate([att, _attn_context(q, k, v, **geo)], axis=0)

        tn_c = _pick(math.gcd(conv_w, four_w, o_a), 512, 256, 128)
        u = _matmul([h], [(w_in_b, 0, 0, o_a // tn_c), (w_in_b, 0, 0, o_b // tn_c)], [],
                    lambda dots: dots[0] * jax.nn.sigmoid(dots[1]),
                    m=m, n=conv_w, tm=tm, tn=tn_c, out_dtype=F32, name="glu_proj")
        conv = _conv_ln_silu(u, w_dw[l], b_dw[l][None, :], conv_ln_g[l][None, :],
                             conv_ln_b[l][None, :], m=m, seq=seq, ctx_len=ctx_len,
                             n_lat_rows=m_lat)

        f_re, f_im = _four_proj_dft(h, w_in_b, 0, o_f, chan_dft, m=m, width=four_w, tm=tm)
        four = _position_dft(f_re, f_im, cn_lat, sn_lat, n=seq, row_blk0=0, n_batch=n_batch,
                             width=four_w, scale=float((seq * gw) ** -0.5))
        if not last:
            four_ctx = _position_dft(f_re, f_im, cn_ctx, sn_ctx, n=ctx_len, row_blk0=m_lat // ctx_len,
                                     n_batch=n_batch, width=four_w,
                                     scale=float((ctx_len * gw) ** -0.5))
            four = jnp.concatenate([four, four_ctx], axis=0)

        tn_m = _pick(D, 256, 128)
        d_t, g_t = D // tn_m, o_g // tn_m

        def merge(dots, b0, b1, b2):
            return (jax.nn.sigmoid(dots[0] + b0) * dots[1] + jax.nn.sigmoid(dots[2] + b1) * dots[3]
                    + jax.nn.sigmoid(dots[4] + b2) * dots[5])

        bg = b_gate[l][None, :]
        merged = _matmul(
            [h, att, conv, four],
            [(w_in_b, 0, 0, g_t), (w_attn_o_b, 0, 1, 0),
             (w_in_b, 0, 0, g_t + d_t), (w_conv_o_b, l, 2, 0),
             (w_in_b, 0, 0, g_t + 2 * d_t), (w_four_o_b, l, 3, 0)],
            [("col", bg, 0), ("col", bg, d_t), ("col", bg, 2 * d_t)], merge,
            m=m, n=D, tm=tm, tn=tn_m, out_dtype=BF16, name="gated_merge")

        tn_o = _pick(D, 1024, 512, 256, 128)
        resid = ("tile2", z, z_tail) if z_tail is not None and m > z.shape[0] else ("tile", z)
        z1 = _matmul([merged], [(w_out_b, 0, 0, 0)], [resid, ("mod", mods, 2)],
                     lambda dots, zt, g: zt + g * dots[0],
                     m=m, n=D, tm=tm, tn=tn_o, out_dtype=F32, name="out_proj", **seg)

        w_router = jnp.zeros((D, ROUTER_LANES), F32)
        w_router = w_router.at[:, :n_groups].set(w_grp[l]).at[:, n_groups:n_groups + n_experts].set(w_rt[l])
        b_router = jnp.zeros((1, ROUTER_LANES), F32)
        b_router = b_router.at[0, :n_groups].set(b_grp[l]).at[0, n_groups:n_groups + n_experts].set(b_rt[l])
        xp, logits = _norm_mod_router(z1, norm2_g[l][None, :], mods, 3, 4, w_router, b_router,
                                      m=m, **seg)
        eid, w_top = _route(logits, n_groups, per_group)
        tm_e = _pick(m * TOP_K, 512, 256, 128)
        slot_pos, tile_e, tile_used, n_tiles = _dispatch_plan(eid, n_experts, tm_e)
        xs = _dispatch(xp, slot_pos, n_rows_out=n_tiles * tm_e)
        y_sorted = _experts(xs, w1_b, w3_b, w2_b, tile_e, tile_used, n_tiles=n_tiles, tm=tm_e)
        z = _combine(y_sorted, slot_pos, w_top, z1, mods, 5,
                     final_g[None, :] if last else None, m=m, **seg)
        z_tail = None
        if not last:
            w_in_b, w_out_b, w_attn_o_b = (w[None] for w in casts[3:])

    return z.reshape(n_batch, seq, D)
```
